```python
import math
import jax, jax.numpy as jnp
from jax import lax
import numpy as np

D_MODEL = 1024
BATCH = 8
SEQ = 2048
DEPTH = 2

HEAD_DIM = 128
HEADS_PER_GROUP = 4
ATT_GROUPS = ((128, 1), (512, 4), (2048, 16))
N_GROUPS = len(ATT_GROUPS)
ATT_WIDTH = N_GROUPS * HEADS_PER_GROUP * HEAD_DIM
ATT_OUT = HEADS_PER_GROUP * HEAD_DIM
ROPE_DIM = HEAD_DIM // 4
ROPE_THETA = 500000.0
NEG_INF = -1e30
HYENA_WIDTH = D_MODEL
HYENA_ORDER = 2
SHORT_CONV = 3
FILTER_EMB = 33
FILTER_HIDDEN = 64
FILTER_INIT_SCALE = 0.05
DECAY_TARGET = 1e-2
FAST_DECAY_PCT = 0.3
SLOW_DECAY_PCT = 1.5
IN_WIDTH = 3 * ATT_WIDTH + 3 * HYENA_WIDTH
N_BRANCH = 2
D_FF = ((8 * D_MODEL // 3 + 255) // 256) * 256
RMS_EPS = 1e-6

kernel_name = "hybrid_dilated_attn_hyena_block"


def rms_norm(x, g):
    xf = x.astype(jnp.float32)
    y = xf * lax.rsqrt(jnp.mean(xf * xf, axis=-1, keepdims=True) + RMS_EPS)
    return (y * g.astype(jnp.float32)).astype(x.dtype)


def partial_rotary(x, pos):
    S = x.shape[1]
    half = ROPE_DIM // 2
    inv = ROPE_THETA ** (-jnp.arange(0, ROPE_DIM, 2, dtype=jnp.float32) / ROPE_DIM)
    ang = pos.astype(jnp.float32)[:, None] * inv[None, :]
    bshape = (1, S) + (1,) * (x.ndim - 3) + (half,)
    cos, sin = jnp.cos(ang).reshape(bshape), jnp.sin(ang).reshape(bshape)
    xr = x[..., :ROPE_DIM].astype(jnp.float32)
    x1, x2 = xr[..., :half], xr[..., half:]
    rot = jnp.concatenate([x1 * cos - x2 * sin, x2 * cos + x1 * sin], axis=-1).astype(x.dtype)
    return jnp.concatenate([rot, x[..., ROPE_DIM:]], axis=-1)


def dilated_window_attention(q, k, v, dilation, nbr):
    B, S, H, hd = q.shape
    Ls = S // dilation
    nb = -(-Ls // nbr)
    Lp = nb * nbr

    def to_sub(t):
        t = t.reshape(B, Ls, dilation, H, hd).transpose(0, 2, 3, 1, 4)
        t = jnp.pad(t, ((0, 0), (0, 0), (0, 0), (0, Lp - Ls), (0, 0)))
        return t.reshape(B, dilation, H, nb, nbr, hd).astype(jnp.float32)

    qs, ks, vs = to_sub(q), to_sub(k), to_sub(v)

    def band(t):
        z = jnp.zeros_like(t[:, :, :, :1])
        tp = jnp.concatenate([z, t, z], axis=3)
        return jnp.concatenate([tp[:, :, :, :-2], tp[:, :, :, 1:-1], tp[:, :, :, 2:]], axis=4)

    kb, vb = band(ks), band(vs)
    qi = jnp.arange(nb)[:, None] * nbr + jnp.arange(nbr)[None, :]
    kj = (jnp.arange(nb)[:, None] - 1) * nbr + jnp.arange(3 * nbr)[None, :]
    valid = ((jnp.abs(qi[:, :, None] - kj[:, None, :]) <= nbr)
             & (kj[:, None, :] >= 0) & (kj[:, None, :] < Ls))
    s = jnp.einsum('brhnqd,brhnkd->brhnqk', qs, kb) * (hd ** -0.5)
    s = jnp.where(valid, s, NEG_INF)
    lse = jax.nn.logsumexp(s, axis=-1)
    p = jnp.exp(s - lse[..., None])
    o = jnp.einsum('brhnqk,brhnkd->brhnqd', p, vb)
    o = o.reshape(B, dilation, H, Lp, hd)[:, :, :, :Ls].transpose(0, 3, 1, 2, 4).reshape(B, S, H, hd)
    lse = lse.reshape(B, dilation, H, Lp)[..., :Ls].transpose(0, 3, 1, 2).reshape(B, S, H)
    return o, lse


def short_conv(u, w, b):
    C = u.shape[-1]
    y = lax.conv_general_dilated(u, w[:, None, :].astype(u.dtype), window_strides=(1,), padding='SAME',
                                 dimension_numbers=('NWC', 'WIO', 'NWC'), feature_group_count=C)
    return y + b.astype(u.dtype)


def filter_features(L):
    bands = (FILTER_EMB - 1) // 2
    t = jnp.linspace(0.0, 1.0, L, dtype=jnp.float32)[:, None]
    w = 2.0 * math.pi * jnp.arange(L, dtype=jnp.float32)[:, None] / L
    f = jnp.linspace(1e-4, bands - 1, bands, dtype=jnp.float32)[None, :]
    return jnp.concatenate([t, jnp.cos(f * w), -jnp.sin(f * w)], axis=-1), t[:, 0]


def implicit_filters(L, w1, b1, fr1, w2, b2, fr2, w3):
    z, t = filter_features(L)
    h = jnp.sin(fr1.astype(jnp.float32) * (z @ w1.astype(jnp.float32) + b1.astype(jnp.float32)))
    h = jnp.sin(fr2.astype(jnp.float32) * (h @ w2.astype(jnp.float32) + b2.astype(jnp.float32)))
    h = (h @ w3.astype(jnp.float32)).reshape(L, 2 * HYENA_ORDER, HYENA_WIDTH).transpose(1, 2, 0)
    deltas = jnp.linspace(math.log(DECAY_TARGET) / SLOW_DECAY_PCT, math.log(DECAY_TARGET) / FAST_DECAY_PCT,
                          HYENA_WIDTH, dtype=jnp.float32)
    decay = jnp.exp(-t[None, :] * jnp.abs(deltas)[:, None])
    return h * decay[None]


def bidir_fft_conv(z, h_fwd, h_bwd, bias):
    B, L, C = z.shape
    k = jnp.concatenate([h_fwd[:, :1] + h_bwd[:, :1], h_fwd[:, 1:], jnp.zeros_like(h_fwd[:, :1]),
                         h_bwd[:, :0:-1]], axis=-1)
    kf = jnp.fft.rfft(k, axis=-1).T
    zf32 = z.astype(jnp.float32)
    zf = jnp.fft.rfft(zf32, n=2 * L, axis=1)
    y = jnp.fft.irfft(zf * kf[None], n=2 * L, axis=1)[:, :L]
    return (y + zf32 * bias.astype(jnp.float32)).astype(z.dtype)


def hybrid_mixer(u, pos, w_in, conv_w, conv_b, fw1, fb1, ff1, fw2, fb2, ff2, fw3, hy_bias,
                 w_o_attn, w_o_hyena, w_gate, b_gate, w_out):
    B, S, _ = u.shape
    proj = u @ w_in
    qkv = proj[..., :3 * ATT_WIDTH].reshape(B, S, 3, N_GROUPS, HEADS_PER_GROUP, HEAD_DIM)
    q = partial_rotary(qkv[:, :, 0], pos)
    k = partial_rotary(qkv[:, :, 1], pos)
    v = qkv[:, :, 2]
    outs, lses = [], []
    for g, (window, dil) in enumerate(ATT_GROUPS):
        o, l = dilated_window_attention(q[:, :, g], k[:, :, g], v[:, :, g], dil, window // (2 * dil))
        outs.append(o)
        lses.append(l)
    alpha = jax.nn.softmax(jnp.stack(lses, axis=2), axis=2)
    o_att = jnp.einsum('bsgh,bsghd->bshd', alpha, jnp.stack(outs, axis=2)).reshape(B, S, ATT_OUT)
    y_att = o_att.astype(u.dtype) @ w_o_attn
    hz = short_conv(proj[..., 3 * ATT_WIDTH:], conv_w, conv_b)
    hv, hx1, hx2 = jnp.split(hz, 3, axis=-1)
    filt = implicit_filters(S, fw1, fb1, ff1, fw2, fb2, ff2, fw3)
    z = hx1 * bidir_fft_conv(hv, filt[0], filt[1], hy_bias[0])
    z = hx2 * bidir_fft_conv(z, filt[2], filt[3], hy_bias[1])
    y_hy = z @ w_o_hyena
    gates = jax.nn.sigmoid((u @ w_gate + b_gate).astype(jnp.float32)).reshape(B, S, N_BRANCH, D_MODEL)
    merged = gates[:, :, 0] * y_att.astype(jnp.float32) + gates[:, :, 1] * y_hy.astype(jnp.float32)
    return merged.astype(u.dtype) @ w_out


def swiglu(u, w_gu, w_down):
    a, b = jnp.split(u @ w_gu, 2, axis=-1)
    return (jax.nn.silu(a) * b) @ w_down


def setup_inputs(seed: int = 0) -> dict:
    key = jax.random.key(seed)
    ks = jax.random.split(key, 24)

    def nrm(k, shape, scale):
        return jax.random.normal(k, shape, jnp.float32) * scale

    def gain(k):
        return 1.0 + nrm(k, (DEPTH, D_MODEL), 0.05)

    return {
        "x": nrm(ks[0], (BATCH, SEQ, D_MODEL), 1.0),
        "norm_mix_pre": gain(ks[1]),
        "norm_mix_post": gain(ks[2]),
        "norm_ffn_pre": gain(ks[3]),
        "norm_ffn_post": gain(ks[4]),
        "w_in": nrm(ks[5], (DEPTH, D_MODEL, IN_WIDTH), D_MODEL ** -0.5),
        "conv_w": nrm(ks[6], (DEPTH, SHORT_CONV, 3 * HYENA_WIDTH), SHORT_CONV ** -0.5),
        "conv_b": nrm(ks[7], (DEPTH, 3 * HYENA_WIDTH), 0.02),
        "filt_w1": nrm(ks[8], (DEPTH, FILTER_EMB, FILTER_HIDDEN), FILTER_EMB ** -0.5),
        "filt_b1": nrm(ks[9], (DEPTH, FILTER_HIDDEN), 0.02),
        "filt_freq1": 1.0 + nrm(ks[10], (DEPTH, FILTER_HIDDEN), 0.1),
        "filt_w2": nrm(ks[11], (DEPTH, FILTER_HIDDEN, FILTER_HIDDEN), FILTER_HIDDEN ** -0.5),
        "filt_b2": nrm(ks[12], (DEPTH, FILTER_HIDDEN), 0.02),
        "filt_freq2": 1.0 + nrm(ks[13], (DEPTH, FILTER_HIDDEN), 0.1),
        "filt_w3": nrm(ks[14], (DEPTH, FILTER_HIDDEN, 2 * HYENA_ORDER * HYENA_WIDTH),
                       FILTER_HIDDEN ** -0.5 * FILTER_INIT_SCALE),
        "hyena_bias": nrm(ks[15], (DEPTH, HYENA_ORDER, HYENA_WIDTH), 1.0),
        "w_o_attn": nrm(ks[16], (DEPTH, ATT_OUT, D_MODEL), ATT_OUT ** -0.5),
        "w_o_hyena": nrm(ks[17], (DEPTH, HYENA_WIDTH, D_MODEL), HYENA_WIDTH ** -0.5),
        "w_gate": nrm(ks[18], (DEPTH, D_MODEL, N_BRANCH * D_MODEL), D_MODEL ** -0.5),
        "b_gate": nrm(ks[19], (DEPTH, N_BRANCH * D_MODEL), 0.02),
        "w_out": nrm(ks[20], (DEPTH, D_MODEL, D_MODEL), D_MODEL ** -0.5),
        "w_gate_up": nrm(ks[21], (DEPTH, D_MODEL, 2 * D_FF), D_MODEL ** -0.5),
        "w_down": nrm(ks[22], (DEPTH, D_FF, D_MODEL), D_FF ** -0.5),
    }


def reference(x, norm_mix_pre, norm_mix_post, norm_ffn_pre, norm_ffn_post, w_in, conv_w, conv_b,
              filt_w1, filt_b1, filt_freq1, filt_w2, filt_b2, filt_freq2, filt_w3, hyena_bias,
              w_o_attn, w_o_hyena, w_gate, b_gate, w_out, w_gate_up, w_down):
    pos = jnp.arange(x.shape[1], dtype=jnp.int32)
    for l in range(DEPTH):
        u = rms_norm(x, norm_mix_pre[l])
        m = hybrid_mixer(u, pos, w_in[l], conv_w[l], conv_b[l], filt_w1[l], filt_b1[l], filt_freq1[l],
                         filt_w2[l], filt_b2[l], filt_freq2[l], filt_w3[l], hyena_bias[l],
                         w_o_attn[l], w_o_hyena[l], w_gate[l], b_gate[l], w_out[l])
        x = x + rms_norm(m, norm_mix_post[l])
        u = rms_norm(x, norm_ffn_pre[l])
        x = x + rms_norm(swiglu(u, w_gate_up[l], w_down[l]), norm_ffn_post[l])
    return x
```

```python
import functools
import math

import numpy as np
import jax
import jax.numpy as jnp
from jax import lax
from jax.experimental import pallas as pl
from jax.experimental.pallas import tpu as pltpu

F32 = jnp.float32
BF16 = jnp.bfloat16

HEAD_DIM = 128
HEADS = 4
ATT_GROUPS = ((128, 1), (512, 4), (2048, 16))
N_GROUPS = len(ATT_GROUPS)
GROUP_W = HEADS * HEAD_DIM
ATT_W = N_GROUPS * GROUP_W
ROPE_DIM = HEAD_DIM // 4
ROPE_HALF = ROPE_DIM // 2
ROPE_THETA = 500000.0
NEG_INF = -1e30
FILTER_EMB = 33
FILTER_HIDDEN = 64
HYENA_ORDER = 2
DECAY_TARGET = 1e-2
FAST_DECAY_PCT = 0.3
SLOW_DECAY_PCT = 1.5
RMS_EPS = 1e-6

LANES = 128
SUBLANES = 8
VMEM_LIMIT_BYTES = 56 * 1024 * 1024

DFT_N2 = 8
Q_BLOCK = 128
HALF_WIN = 64


def _params(*sem):
    return pltpu.CompilerParams(dimension_semantics=sem, vmem_limit_bytes=VMEM_LIMIT_BYTES)


def _rms(x, gain):
    return x * lax.rsqrt(jnp.mean(x * x, axis=-1, keepdims=True) + RMS_EPS) * gain


def _proj_kernel(x_ref, g_ref, w_ref, o_ref, u_ref, *, col_axis):
    @pl.when(pl.program_id(col_axis) == 0)
    def _():
        x = x_ref[...].reshape(u_ref.shape)
        u_ref[...] = _rms(x, g_ref[...]).astype(BF16)

    res = jnp.dot(u_ref[...], w_ref[...], preferred_element_type=F32)
    o_ref[...] = res.astype(o_ref.dtype).reshape(o_ref.shape)


def _proj_attention(x, gain, w_bf16, *, tm, tn):
    B, S, D = x.shape
    T = B * S
    ncol = 3 * ATT_W
    return pl.pallas_call(
        functools.partial(_proj_kernel, col_axis=1),
        grid=(T // tm, ncol // tn),
        in_specs=[
            pl.BlockSpec((tm, D), lambda i, j: (i, 0)),
            pl.BlockSpec((1, D), lambda i, j: (0, 0)),
            pl.BlockSpec((D, tn), lambda i, j: (0, j)),
        ],
        out_specs=pl.BlockSpec((tm, tn), lambda i, j: (i, j)),
        out_shape=jax.ShapeDtypeStruct((T, ncol), BF16),
        scratch_shapes=[pltpu.VMEM((tm, D), BF16)],
        compiler_params=_params("parallel", "arbitrary"),
    )(x.reshape(T, D), gain, w_bf16)


def _proj_hyena(x, gain, w_bf16, *, nb, tn):
    B, S, D = x.shape
    n1 = S // DFT_N2
    ncol = w_bf16.shape[1] - 3 * ATT_W
    col0 = 3 * ATT_W // tn
    return pl.pallas_call(
        functools.partial(_proj_kernel, col_axis=2),
        grid=(B // nb, DFT_N2, ncol // tn),
        in_specs=[
            pl.BlockSpec((nb, n1, D), lambda b, j, c: (b, 0, j)),
            pl.BlockSpec((1, D), lambda b, j, c: (0, 0)),
            pl.BlockSpec((D, tn), lambda b, j, c: (0, col0 + c)),
        ],
        out_specs=pl.BlockSpec((nb, None, n1, tn), lambda b, j, c: (b, j, 0, c)),
        out_shape=jax.ShapeDtypeStruct((B, DFT_N2, n1, ncol), BF16),
        scratch_shapes=[pltpu.VMEM((nb * n1, D), BF16)],
        compiler_params=_params("parallel", "parallel", "arbitrary"),
    )(x.reshape(B, n1, DFT_N2 * D), gain, w_bf16)


def _rotary_tables(S):
    pos = jnp.arange(S, dtype=jnp.int32)
    inv = ROPE_THETA ** (-jnp.arange(0, ROPE_DIM, 2, dtype=F32) / ROPE_DIM)
    ang = pos.astype(F32)[:, None] * inv[None, :]
    cos, sin = jnp.cos(ang), jnp.sin(ang)
    rest = HEAD_DIM - ROPE_DIM
    c = jnp.concatenate([cos, cos, jnp.ones((S, rest), F32)], axis=-1)
    sa = jnp.concatenate([-sin, jnp.zeros((S, HEAD_DIM - ROPE_HALF), F32)], axis=-1)
    sb = jnp.concatenate([jnp.zeros((S, ROPE_HALF), F32), sin, jnp.zeros((S, rest), F32)], axis=-1)
    return c, sa, sb


def _rotate(x, c, sa, sb):
    return x * c + pltpu.roll(x, HEAD_DIM - ROPE_HALF, 1) * sa + pltpu.roll(x, ROPE_HALF, 1) * sb


def _attn_kernel(q_ref, k_ref, v_ref, c_ref, sa_ref, sb_ref, o_ref, lse_ref, kr_ref, *, ls):
    kw = min(2 * Q_BLOCK, ls)
    nblk = ls // Q_BLOCK
    scale = HEAD_DIM ** -0.5

    def rot_k(n, carry):
        r0 = pl.multiple_of(n * Q_BLOCK, Q_BLOCK)
        rows = pl.ds(r0, Q_BLOCK)
        c, sa, sb = c_ref[rows, :], sa_ref[rows, :], sb_ref[rows, :]
        for h in range(HEADS):
            cols = slice(h * HEAD_DIM, (h + 1) * HEAD_DIM)
            kr_ref[rows, cols] = _rotate(k_ref[rows, cols].astype(F32), c, sa, sb).astype(BF16)
        return carry

    lax.fori_loop(0, nblk, rot_k, 0)

    row = lax.broadcasted_iota(jnp.int32, (Q_BLOCK, kw), 0)
    col = lax.broadcasted_iota(jnp.int32, (Q_BLOCK, kw), 1)
    rel = col - row

    def block(n, carry):
        i0 = pl.multiple_of(n * Q_BLOCK, Q_BLOCK)
        k0 = pl.multiple_of(jnp.clip(i0 - HALF_WIN, 0, ls - kw), HALF_WIN)
        d = rel + (k0 - i0)
        valid = (d <= HALF_WIN) & (d >= -HALF_WIN)
        qrows = pl.ds(i0, Q_BLOCK)
        krows = pl.ds(k0, kw)
        c, sa, sb = c_ref[qrows, :], sa_ref[qrows, :], sb_ref[qrows, :]
        for h in range(HEADS):
            cols = slice(h * HEAD_DIM, (h + 1) * HEAD_DIM)
            q = (_rotate(q_ref[qrows, cols].astype(F32), c, sa, sb) * scale).astype(BF16)
            s = lax.dot_general(q, kr_ref[krows, cols], (((1,), (1,)), ((), ())),
                                preferred_element_type=F32)
            s = jnp.where(valid, s, NEG_INF)
            m = jnp.max(s, axis=-1, keepdims=True)
            p = jnp.exp(s - m)
            l = jnp.sum(p, axis=-1, keepdims=True)
            o = jnp.dot(p.astype(BF16), v_ref[krows, cols], preferred_element_type=F32)
            o_ref[qrows, cols] = (o / l).astype(o_ref.dtype)
            lse_ref[qrows, cols] = jnp.broadcast_to(m + jnp.log(l), (Q_BLOCK, HEAD_DIM))
        return carry

    lax.fori_loop(0, nblk, block, 0)


def _attention_group(qkv, tables, g, B, S):
    dil = ATT_GROUPS[g][1]
    ls = S // dil
    ncb = 3 * N_GROUPS
    qkv_v = qkv.reshape(B, ls, dil * ncb * GROUP_W)
    tabs = [t.reshape(ls, dil * HEAD_DIM) for t in tables]

    def qkv_spec(which):
        return pl.BlockSpec((None, ls, GROUP_W), lambda b, r: (b, 0, r * ncb + which * N_GROUPS + g))

    tab_spec = pl.BlockSpec((ls, HEAD_DIM), lambda b, r: (0, r))
    out_spec = pl.BlockSpec((None, ls, GROUP_W), lambda b, r: (b, 0, r))
    o, lse = pl.pallas_call(
        functools.partial(_attn_kernel, ls=ls),
        grid=(B, dil),
        in_specs=[qkv_spec(0), qkv_spec(1), qkv_spec(2), tab_spec, tab_spec, tab_spec],
        out_specs=[out_spec, out_spec],
        out_shape=[jax.ShapeDtypeStruct((B, ls, dil * GROUP_W), BF16),
                   jax.ShapeDtypeStruct((B, ls, dil * GROUP_W), F32)],
        scratch_shapes=[pltpu.VMEM((ls, GROUP_W), BF16)],
        compiler_params=_params("parallel", "parallel"),
    )(qkv_v, qkv_v, qkv_v, *tabs)
    return o.reshape(B, S, GROUP_W), lse.reshape(B, S, GROUP_W)


def _dft_matrices(S):
    N = 2 * S
    K = S // DFT_N2
    k1 = np.arange(K, dtype=np.float64)[:, None] + 0.5
    mats = []
    for j in range(DFT_N2):
        n = DFT_N2 * np.arange(K, dtype=np.float64)[None, :] + j
        theta = 2.0 * np.pi * k1 * n / N
        mats.append(np.concatenate([np.cos(theta), -np.sin(theta)], axis=0))
    f = np.stack(mats).astype(np.float32)
    return jnp.asarray(f, dtype=BF16), jnp.asarray(f.transpose(0, 2, 1), dtype=BF16)


def _cmul_const(v, c, s):
    re, im = v
    if (c, s) == (1, 0):
        return re, im
    if (c, s) == (-1, 0):
        return -re, -im
    if (c, s) == (0, 1):
        return -im, re
    if (c, s) == (0, -1):
        return im, -re
    return re * c - im * s, re * s + im * c


def _small_dft(xs, sign):
    n = len(xs)
    if n == 1:
        return xs
    ev = _small_dft(xs[0::2], sign)
    od = _small_dft(xs[1::2], sign)
    out = [None] * n
    for k in range(n // 2):
        c = round(math.cos(2.0 * math.pi * k / n), 15)
        s = round(sign * math.sin(2.0 * math.pi * k / n), 15)
        c = int(c) if c in (-1.0, 0.0, 1.0) else c
        s = int(s) if s in (-1.0, 0.0, 1.0) else s
        tr, ti = _cmul_const(od[k], c, s)
        out[k] = (ev[k][0] + tr, ev[k][1] + ti)
        out[k + n // 2] = (ev[k][0] - tr, ev[k][1] - ti)
    return out


def _load_complex(ref, lead, r0, cols, K):
    re = ref[lead + (pl.ds(r0, SUBLANES), cols)]
    im = ref[lead + (pl.ds(K + r0, SUBLANES), cols)]
    return re, im


def _filter_kernel(z_ref, w1_ref, b1_ref, f1_ref, w2_ref, b2_ref, f2_ref, w3_ref, delta_ref, dft_ref,
                   kf_ref, a_ref, *, K):
    ct = kf_ref.shape[-1]
    n_filt = 2 * HYENA_ORDER
    hi = lax.Precision.HIGHEST
    for j in range(DFT_N2):
        z = z_ref[j]
        h = jnp.sin(f1_ref[...] * (jnp.dot(z, w1_ref[...], precision=hi, preferred_element_type=F32)
                                   + b1_ref[...]))
        h = jnp.sin(f2_ref[...] * (jnp.dot(h, w2_ref[...], precision=hi, preferred_element_type=F32)
                                   + b2_ref[...]))
        decay = jnp.exp(-z[:, 0:1] * jnp.abs(delta_ref[...]))
        for f in range(n_filt):
            hf = jnp.dot(h, w3_ref[f], precision=hi, preferred_element_type=F32) * decay
            a_ref[f, j] = jnp.dot(dft_ref[j], hf.astype(BF16), preferred_element_type=F32)

    inv_scale = 2.0 / (2 * DFT_N2 * K)

    def chunk(i, carry):
        r0 = pl.multiple_of(i * SUBLANES, SUBLANES)
        for lc in range(ct // LANES):
            cols = slice(lc * LANES, (lc + 1) * LANES)
            spec = []
            for f in range(n_filt):
                spec.append(_small_dft([_load_complex(a_ref, (f, j), r0, cols, K) for j in range(DFT_N2)], -1))
            for o in range(HYENA_ORDER):
                fw, bw = spec[2 * o], spec[2 * o + 1]
                for k2 in range(DFT_N2):
                    kf_ref[o, k2, pl.ds(r0, SUBLANES), cols] = (fw[k2][0] + bw[k2][0]) * inv_scale
                    kf_ref[o, k2, pl.ds(K + r0, SUBLANES), cols] = (fw[k2][1] - bw[k2][1]) * inv_scale
        return carry

    lax.fori_loop(0, K // SUBLANES, chunk, 0)


def _filter_spectrum(S, C, fw1, fb1, ff1, fw2, fb2, ff2, fw3, dft, *, ct):
    K = S // DFT_N2
    bands = (FILTER_EMB - 1) // 2
    t = jnp.linspace(0.0, 1.0, S, dtype=F32)[:, None]
    w = 2.0 * math.pi * jnp.arange(S, dtype=F32)[:, None] / S
    fr = jnp.linspace(1e-4, bands - 1, bands, dtype=F32)[None, :]
    feats = jnp.concatenate([t, jnp.cos(fr * w), -jnp.sin(fr * w)], axis=-1)
    feats = jnp.pad(feats, ((0, 0), (0, LANES - FILTER_EMB)))
    feats = feats.reshape(K, DFT_N2, LANES).transpose(1, 0, 2)
    w1 = jnp.pad(fw1, ((0, LANES - FILTER_EMB), (0, 0)))
    w3 = fw3.reshape(FILTER_HIDDEN, 2 * HYENA_ORDER, C).transpose(1, 0, 2)
    deltas = jnp.linspace(math.log(DECAY_TARGET) / SLOW_DECAY_PCT, math.log(DECAY_TARGET) / FAST_DECAY_PCT,
                          C, dtype=F32)[None, :]
    row = lambda v: v.reshape(1, -1)
    const2 = lambda shape: pl.BlockSpec(shape, lambda c: (0, 0))
    return pl.pallas_call(
        functools.partial(_filter_kernel, K=K),
        grid=(C // ct,),
        in_specs=[
            pl.BlockSpec((DFT_N2, K, LANES), lambda c: (0, 0, 0)),
            const2((LANES, FILTER_HIDDEN)), const2((1, FILTER_HIDDEN)), const2((1, FILTER_HIDDEN)),
            const2((FILTER_HIDDEN, FILTER_HIDDEN)), const2((1, FILTER_HIDDEN)), const2((1, FILTER_HIDDEN)),
            pl.BlockSpec((2 * HYENA_ORDER, FILTER_HIDDEN, ct), lambda c: (0, 0, c)),
            pl.BlockSpec((1, ct), lambda c: (0, c)),
            pl.BlockSpec((DFT_N2, 2 * K, K), lambda c: (0, 0, 0)),
        ],
        out_specs=pl.BlockSpec((HYENA_ORDER, DFT_N2, 2 * K, ct), lambda c: (0, 0, 0, c)),
        out_shape=jax.ShapeDtypeStruct((HYENA_ORDER, DFT_N2, 2 * K, C), F32),
        scratch_shapes=[pltpu.VMEM((2 * HYENA_ORDER, DFT_N2, 2 * K, ct), F32)],
        compiler_params=_params("parallel"),
    )(feats, w1, row(fb1), row(ff1), fw2, row(fb2), row(ff2), w3, deltas, dft)


def _short_conv(p_ref, w, b):
    K = p_ref.shape[1]
    p = [p_ref[j].astype(F32) for j in range(DFT_N2)]
    rows = lax.broadcasted_iota(jnp.int32, p[0].shape, 0)
    prev_of_first = jnp.where(rows == 0, 0.0, pltpu.roll(p[DFT_N2 - 1], 1, 0))
    next_of_last = jnp.where(rows == K - 1, 0.0, pltpu.roll(p[0], K - 1, 0))
    out = []
    for j in range(DFT_N2):
        prev = p[j - 1] if j > 0 else prev_of_first
        nxt = p[j + 1] if j < DFT_N2 - 1 else next_of_last
        out.append(prev * w[0:1, :] + p[j] * w[1:2, :] + nxt * w[2:3, :] + b)
    return out


def _hyena_kernel(hv_ref, hx1_ref, hx2_ref, cw_ref, cb_ref, bias_ref, kf_ref, dft_ref, dftt_ref,
                  o_ref, sig_ref, g1_ref, g2_ref, a_ref, *, K):
    ct = o_ref.shape[-1]
    cw = cw_ref[...]
    cb = cb_ref[...]
    for idx, (src, dst) in enumerate(((hv_ref, sig_ref), (hx1_ref, g1_ref), (hx2_ref, g2_ref))):
        cols = slice(idx * ct, (idx + 1) * ct)
        for j, v in enumerate(_short_conv(src, cw[:, cols], cb[:, cols])):
            dst[j] = v

    def long_conv(order):
        for j in range(DFT_N2):
            a_ref[j] = jnp.dot(dft_ref[j], sig_ref[j].astype(BF16), preferred_element_type=F32)

        def chunk(i, carry):
            r0 = pl.multiple_of(i * SUBLANES, SUBLANES)
            for lc in range(ct // LANES):
                cols = slice(lc * LANES, (lc + 1) * LANES)
                spec = _small_dft([_load_complex(a_ref, (j,), r0, cols, K) for j in range(DFT_N2)], -1)
                prod = []
                for k2 in range(DFT_N2):
                    kr, ki = _load_complex(kf_ref, (order, k2), r0, cols, K)
                    xr, xi = spec[k2]
                    prod.append((xr * kr - xi * ki, xr * ki + xi * kr))
                back = _small_dft(prod, +1)
                for j in range(DFT_N2):
                    a_ref[j, pl.ds(r0, SUBLANES), cols] = back[j][0]
                    a_ref[j, pl.ds(K + r0, SUBLANES), cols] = back[j][1]
            return carry

        lax.fori_loop(0, K // SUBLANES, chunk, 0)
        return [jnp.dot(dftt_ref[j], a_ref[j].astype(BF16), preferred_element_type=F32) for j in range(DFT_N2)]

    bias = bias_ref[...]
    y = long_conv(0)
    for j in range(DFT_N2):
        sig_ref[j] = g1_ref[j] * (y[j] + bias[0:1, :] * sig_ref[j])
    y = long_conv(1)
    for j in range(DFT_N2):
        o_ref[j] = (g2_ref[j] * (y[j] + bias[1:2, :] * sig_ref[j])).astype(o_ref.dtype)


def _hyena_branch(hz, conv_w, conv_b, hy_bias, kf, dft, dftt, *, ct):
    B, _, K, c3 = hz.shape
    C = c3 // 3
    nct = C // ct
    cw = conv_w.reshape(3, 3, nct, ct).transpose(0, 2, 1, 3).reshape(3, nct * 3 * ct)
    cb = conv_b.reshape(3, nct, ct).transpose(1, 0, 2).reshape(1, nct * 3 * ct)
    hz_spec = lambda part: pl.BlockSpec((None, DFT_N2, K, ct), lambda c, b: (b, 0, 0, part * nct + c))
    work = pltpu.VMEM((DFT_N2, K, ct), F32)
    return pl.pallas_call(
        functools.partial(_hyena_kernel, K=K),
        grid=(nct, B),
        in_specs=[
            hz_spec(0), hz_spec(1), hz_spec(2),
            pl.BlockSpec((3, 3 * ct), lambda c, b: (0, c)),
            pl.BlockSpec((1, 3 * ct), lambda c, b: (0, c)),
            pl.BlockSpec((HYENA_ORDER, ct), lambda c, b: (0, c)),
            pl.BlockSpec((HYENA_ORDER, DFT_N2, 2 * K, ct), lambda c, b: (0, 0, 0, c)),
            pl.BlockSpec((DFT_N2, 2 * K, K), lambda c, b: (0, 0, 0), pipeline_mode=pl.Buffered(1)),
            pl.BlockSpec((DFT_N2, K, 2 * K), lambda c, b: (0, 0, 0), pipeline_mode=pl.Buffered(1)),
        ],
        out_specs=pl.BlockSpec((None, DFT_N2, K, ct), lambda c, b: (b, 0, 0, c)),
        out_shape=jax.ShapeDtypeStruct((B, DFT_N2, K, C), BF16),
        scratch_shapes=[work, work, work, pltpu.VMEM((DFT_N2, 2 * K, ct), F32)],
        compiler_params=_params("parallel", "arbitrary"),
    )(hz, hz, hz, cw, cb, hy_bias, kf, dft, dftt)


def _merge_kernel(x_ref, gpre_ref, o0_ref, o1_ref, o2_ref, l0_ref, l1_ref, l2_ref, z_ref,
                  wg_ref, bg_ref, woa_ref, woh_ref, wout_ref, gpost_ref, out_ref):
    D = x_ref.shape[-1]
    tm = x_ref.shape[0] * x_ref.shape[1]
    x = x_ref[...].reshape(tm, D)
    u = _rms(x, gpre_ref[...]).astype(BF16)

    lses = [r[...].reshape(tm, GROUP_W) for r in (l0_ref, l1_ref, l2_ref)]
    outs = [r[...].reshape(tm, GROUP_W).astype(F32) for r in (o0_ref, o1_ref, o2_ref)]
    m = jnp.maximum(jnp.maximum(lses[0], lses[1]), lses[2])
    e = [jnp.exp(l - m) for l in lses]
    o_att = (e[0] * outs[0] + e[1] * outs[1] + e[2] * outs[2]) / (e[0] + e[1] + e[2])
    y_att = jnp.dot(o_att.astype(BF16), woa_ref[...], preferred_element_type=F32)
    y_hy = jnp.dot(z_ref[...].reshape(tm, -1), woh_ref[...], preferred_element_type=F32)

    gate = lambda half: jax.nn.sigmoid(
        jnp.dot(u, wg_ref[:, half * D:(half + 1) * D], preferred_element_type=F32)
        + bg_ref[:, half * D:(half + 1) * D])
    merged = gate(0) * y_att + gate(1) * y_hy
    mix = jnp.dot(merged.astype(BF16), wout_ref[...], preferred_element_type=F32)
    out_ref[...] = (x + _rms(mix, gpost_ref[...])).reshape(out_ref.shape)


def _merge(x, gpre, o_groups, lse_groups, z, wg, bg, woa, woh, wout, gpost, *, nb):
    B, S, D = x.shape
    K = S // DFT_N2
    C = z.shape[-1]
    tok = lambda width: pl.BlockSpec((nb, K, width), lambda b, j: (b, 0, j))
    full = lambda a: pl.BlockSpec(a.shape, lambda b, j: (0,) * a.ndim, pipeline_mode=pl.Buffered(1))
    view = lambda a: a.reshape(B, K, DFT_N2 * a.shape[-1])
    out = pl.pallas_call(
        _merge_kernel,
        grid=(B // nb, DFT_N2),
        in_specs=[tok(D), full(gpre)] + [tok(GROUP_W)] * 6
                 + [pl.BlockSpec((nb, None, K, C), lambda b, j: (b, j, 0, 0))]
                 + [full(a) for a in (wg, bg, woa, woh, wout, gpost)],
        out_specs=tok(D),
        out_shape=jax.ShapeDtypeStruct((B, K, DFT_N2 * D), F32),
        compiler_params=_params("parallel", "parallel"),
    )(view(x), gpre, *[view(o) for o in o_groups], *[view(l) for l in lse_groups], z,
      wg, bg, woa, woh, wout, gpost)
    return out.reshape(B, S, D)


def _ffn_kernel(x_ref, gpre_ref, wgu_ref, wdown_ref, gpost_ref, out_ref, *, n_chunks):
    x = x_ref[...]
    u = _rms(x, gpre_ref[...]).astype(BF16)
    dff = wdown_ref.shape[0]
    ck = dff // n_chunks
    acc = jnp.zeros(x.shape, F32)
    for c in range(n_chunks):
        a = jnp.dot(u, wgu_ref[:, c * ck:(c + 1) * ck], preferred_element_type=F32)
        b = jnp.dot(u, wgu_ref[:, dff + c * ck:dff + (c + 1) * ck], preferred_element_type=F32)
        h = (a * jax.nn.sigmoid(a) * b).astype(BF16)
        acc = acc + jnp.dot(h, wdown_ref[c * ck:(c + 1) * ck, :], preferred_element_type=F32)
    out_ref[...] = x + _rms(acc, gpost_ref[...])


def _ffn(x, gpre, wgu, wdown, gpost, *, tm, n_chunks):
    B, S, D = x.shape
    T = B * S
    full = lambda a: pl.BlockSpec(a.shape, lambda i: (0,) * a.ndim, pipeline_mode=pl.Buffered(1))
    out = pl.pallas_call(
        functools.partial(_ffn_kernel, n_chunks=n_chunks),
        grid=(T // tm,),
        in_specs=[pl.BlockSpec((tm, D), lambda i: (i, 0)), full(gpre), full(wgu), full(wdown), full(gpost)],
        out_specs=pl.BlockSpec((tm, D), lambda i: (i, 0)),
        out_shape=jax.ShapeDtypeStruct((T, D), F32),
        compiler_params=_params("parallel"),
    )(x.reshape(T, D), gpre, wgu, wdown, gpost)
    return out.reshape(B, S, D)


def kernel(x, norm_mix_pre, norm_mix_post, norm_ffn_pre, norm_ffn_post, w_in, conv_w, conv_b, filt_w1, filt_b1, filt_freq1, filt_w2, filt_b2, filt_freq2, filt_w3, hyena_bias, w_o_attn, w_o_hyena, w_gate, b_gate, w_out, w_gate_up, w_down):
    B, S, D = x.shape
    depth = w_in.shape[0]
    C = w_o_hyena.shape[1]
    assert S % (DFT_N2 * SUBLANES) == 0 and w_in.shape[2] == 3 * ATT_W + 3 * C

    tables = _rotary_tables(S)
    dft, dftt = _dft_matrices(S)
    row = lambda v: v.reshape(1, -1)

    for l in range(depth):
        w_in_l = w_in[l].astype(BF16)
        g_pre = row(norm_mix_pre[l])
        qkv = _proj_attention(x, g_pre, w_in_l, tm=1024, tn=768)
        hz = _proj_hyena(x, g_pre, w_in_l, nb=4, tn=768)

        groups = [_attention_group(qkv, tables, g, B, S) for g in range(N_GROUPS)]
        kf = _filter_spectrum(S, C, filt_w1[l], filt_b1[l], filt_freq1[l], filt_w2[l], filt_b2[l],
                              filt_freq2[l], filt_w3[l], dft, ct=128)
        z = _hyena_branch(hz, conv_w[l], conv_b[l], hyena_bias[l], kf, dft, dftt, ct=256)

        x = _merge(x, g_pre, [o for o, _ in groups], [s for _, s in groups], z,
                   w_gate[l].astype(BF16), row(b_gate[l]), w_o_attn[l].astype(BF16),
                   w_o_hyena[l].astype(BF16), w_out[l].astype(BF16), row(norm_mix_post[l]), nb=2)
        x = _ffn(x, row(norm_ffn_pre[l]), w_gate_up[l].astype(BF16), w_down[l].astype(BF16),
                 row(norm_ffn_post[l]), tm=512, n_chunks=2)
    return x
```

```python
import functools
import math

import numpy as np
import jax
import jax.numpy as jnp
from jax import lax
from jax.experimental import pallas as pl
from jax.experimental.pallas import tpu as pltpu

F32 = jnp.float32
BF16 = jnp.bfloat16

HEAD_DIM = 128
HEADS = 4
ATT_GROUPS = ((128, 1), (512, 4), (2048, 16))
N_GROUPS = len(ATT_GROUPS)
GROUP_W = HEADS * HEAD_DIM
ATT_W = N_GROUPS * GROUP_W
ROPE_DIM = HEAD_DIM // 4
ROPE_HALF = ROPE_DIM // 2
ROPE_THETA = 500000.0
NEG_INF = -1e30
FILTER_EMB = 33
FILTER_HIDDEN = 64
HYENA_ORDER = 2
DECAY_TARGET = 1e-2
FAST_DECAY_PCT = 0.3
SLOW_DECAY_PCT = 1.5
RMS_EPS = 1e-6

LANES = 128
SUBLANES = 8
VMEM_LIMIT_BYTES = 56 * 1024 * 1024

DFT_N2 = 8
Q_BLOCK = 128
HALF_WIN = 64
PERM_TILE = 256
LSE_LANES = HEAD_DIM // HEADS


def _params(*sem):
    return pltpu.CompilerParams(dimension_semantics=sem, vmem_limit_bytes=VMEM_LIMIT_BYTES)


def _rms(x, gain):
    return x * lax.rsqrt(jnp.mean(x * x, axis=-1, keepdims=True) + RMS_EPS) * gain


def _perm_matrix(dil):
    rows = PERM_TILE // dil
    p = np.zeros((PERM_TILE, PERM_TILE), np.float32)
    for r in range(dil):
        for t in range(rows):
            p[r * rows + t, dil * t + r] = 1.0
    return p


def _proj_kernel(x_ref, g_ref, p_ref, w_ref, o_ref, u_ref, *, dil):
    S, D = x_ref.shape
    ls = S // dil
    rows = PERM_TILE // dil

    @pl.when(pl.program_id(1) == 0)
    def _():
        for k in range(S // PERM_TILE):
            u = _rms(x_ref[k * PERM_TILE:(k + 1) * PERM_TILE, :], g_ref[...]).astype(BF16)
            if dil == 1:
                u_ref[k * PERM_TILE:(k + 1) * PERM_TILE, :] = u
            else:
                up = jnp.dot(p_ref[...], u, preferred_element_type=F32).astype(BF16)
                for r in range(dil):
                    u_ref[r * ls + k * rows:r * ls + (k + 1) * rows, :] = up[r * rows:(r + 1) * rows]

    res = jnp.dot(u_ref[...], w_ref[...], preferred_element_type=F32)
    o_ref[...] = res.astype(o_ref.dtype)


def _proj(x, gain, w_bf16, *, dil, tn, col_blocks):
    B, S, D = x.shape
    n = len(col_blocks)
    c0, step = col_blocks[0], (col_blocks[1] - col_blocks[0] if n > 1 else 0)
    perm = jnp.asarray(_perm_matrix(dil), dtype=BF16)
    return pl.pallas_call(
        functools.partial(_proj_kernel, dil=dil),
        grid=(B, n),
        in_specs=[
            pl.BlockSpec((None, S, D), lambda b, c: (b, 0, 0)),
            pl.BlockSpec((1, D), lambda b, c: (0, 0)),
            pl.BlockSpec((PERM_TILE, PERM_TILE), lambda b, c: (0, 0)),
            pl.BlockSpec((D, tn), lambda b, c: (0, c0 + step * c)),
        ],
        out_specs=pl.BlockSpec((None, S, tn), lambda b, c: (b, 0, c)),
        out_shape=jax.ShapeDtypeStruct((B, S, n * tn), BF16),
        scratch_shapes=[pltpu.VMEM((S, D), BF16)],
        compiler_params=_params("parallel", "arbitrary"),
    )(x, gain, perm, w_bf16)


def _rotary_tables(S):
    pos = jnp.arange(S, dtype=jnp.int32)
    inv = ROPE_THETA ** (-jnp.arange(0, ROPE_DIM, 2, dtype=F32) / ROPE_DIM)
    ang = pos.astype(F32)[:, None] * inv[None, :]
    cos, sin = jnp.cos(ang), jnp.sin(ang)
    rest = HEAD_DIM - ROPE_DIM
    c = jnp.concatenate([cos, cos, jnp.ones((S, rest), F32)], axis=-1)
    sa = jnp.concatenate([-sin, jnp.zeros((S, HEAD_DIM - ROPE_HALF), F32)], axis=-1)
    sb = jnp.concatenate([jnp.zeros((S, ROPE_HALF), F32), sin, jnp.zeros((S, rest), F32)], axis=-1)
    return c, sa, sb


def _rotate(x, c, sa, sb):
    return x * c + pltpu.roll(x, HEAD_DIM - ROPE_HALF, 1) * sa + pltpu.roll(x, ROPE_HALF, 1) * sb


def _attn_kernel(q_ref, k_ref, v_ref, c_ref, sa_ref, sb_ref, o_ref, lse_ref, kr_ref, *, ls):
    S = q_ref.shape[0]
    kw = min(2 * Q_BLOCK, ls)
    blocks_per_seq = ls // Q_BLOCK
    nblk = S // Q_BLOCK
    scale = HEAD_DIM ** -0.5

    def rot_k(n, carry):
        r0 = pl.multiple_of(n * Q_BLOCK, Q_BLOCK)
        rows = pl.ds(r0, Q_BLOCK)
        c, sa, sb = c_ref[rows, :], sa_ref[rows, :], sb_ref[rows, :]
        for h in range(HEADS):
            cols = slice(h * HEAD_DIM, (h + 1) * HEAD_DIM)
            kr_ref[rows, cols] = _rotate(k_ref[rows, cols].astype(F32), c, sa, sb).astype(BF16)
        return carry

    lax.fori_loop(0, nblk, rot_k, 0)

    row = lax.broadcasted_iota(jnp.int32, (Q_BLOCK, kw), 0)
    col = lax.broadcasted_iota(jnp.int32, (Q_BLOCK, kw), 1)
    rel = col - row
    lane = lax.broadcasted_iota(jnp.int32, (Q_BLOCK, HEAD_DIM), 1)

    def block(n, carry):
        base = (n // blocks_per_seq) * ls
        i0 = pl.multiple_of(n * Q_BLOCK, Q_BLOCK)
        k0 = pl.multiple_of(base + jnp.clip(i0 - base - HALF_WIN, 0, ls - kw), HALF_WIN)
        d = rel + (k0 - i0)
        valid = (d <= HALF_WIN) & (d >= -HALF_WIN)
        qrows = pl.ds(i0, Q_BLOCK)
        krows = pl.ds(k0, kw)
        c, sa, sb = c_ref[qrows, :], sa_ref[qrows, :], sb_ref[qrows, :]
        lse_tile = jnp.zeros((Q_BLOCK, HEAD_DIM), F32)
        for h in range(HEADS):
            cols = slice(h * HEAD_DIM, (h + 1) * HEAD_DIM)
            q = (_rotate(q_ref[qrows, cols].astype(F32), c, sa, sb) * scale).astype(BF16)
            s = lax.dot_general(q, kr_ref[krows, cols], (((1,), (1,)), ((), ())),
                                preferred_element_type=F32)
            s = jnp.where(valid, s, NEG_INF)
            m = jnp.max(s, axis=-1, keepdims=True)
            p = jnp.exp(s - m)
            l = jnp.sum(p, axis=-1, keepdims=True)
            o = jnp.dot(p.astype(BF16), v_ref[krows, cols], preferred_element_type=F32)
            o_ref[qrows, cols] = (o / l).astype(o_ref.dtype)
            lse_tile = jnp.where(lane // LSE_LANES == h, m + jnp.log(l), lse_tile)
        lse_ref[qrows, :] = lse_tile
        return carry

    lax.fori_loop(0, nblk, block, 0)


def _attention_group(qkv, tables, dil):
    B, S, _ = qkv.shape
    ls = S // dil
    tabs = [t.reshape(ls, dil, HEAD_DIM).transpose(1, 0, 2).reshape(S, HEAD_DIM) for t in tables]
    part = lambda which: pl.BlockSpec((None, S, GROUP_W), lambda b: (b, 0, which))
    tab_spec = pl.BlockSpec((S, HEAD_DIM), lambda b: (0, 0))
    return pl.pallas_call(
        functools.partial(_attn_kernel, ls=ls),
        grid=(B,),
        in_specs=[part(0), part(1), part(2), tab_spec, tab_spec, tab_spec],
        out_specs=[pl.BlockSpec((None, S, GROUP_W), lambda b: (b, 0, 0)),
                   pl.BlockSpec((None, S, HEAD_DIM), lambda b: (b, 0, 0))],
        out_shape=[jax.ShapeDtypeStruct((B, S, GROUP_W), BF16),
                   jax.ShapeDtypeStruct((B, S, HEAD_DIM), F32)],
        scratch_shapes=[pltpu.VMEM((S, GROUP_W), BF16)],
        compiler_params=_params("parallel"),
    )(qkv, qkv, qkv, *tabs)


def _dft_matrices(S):
    N = 2 * S
    K = S // DFT_N2
    k1 = np.arange(K, dtype=np.float64)[:, None] + 0.5
    mats = []
    for j in range(DFT_N2):
        n = DFT_N2 * np.arange(K, dtype=np.float64)[None, :] + j
        theta = 2.0 * np.pi * k1 * n / N
        mats.append(np.concatenate([np.cos(theta), -np.sin(theta)], axis=0))
    f = np.stack(mats).astype(np.float32)
    return jnp.asarray(f).astype(BF16), jnp.asarray(f.transpose(0, 2, 1)).astype(BF16)


def _cmul_const(v, c, s):
    re, im = v
    if (c, s) == (1, 0):
        return re, im
    if (c, s) == (-1, 0):
        return -re, -im
    if (c, s) == (0, 1):
        return -im, re
    if (c, s) == (0, -1):
        return im, -re
    return re * c - im * s, re * s + im * c


def _small_dft(xs, sign):
    n = len(xs)
    if n == 1:
        return xs
    ev = _small_dft(xs[0::2], sign)
    od = _small_dft(xs[1::2], sign)
    out = [None] * n
    for k in range(n // 2):
        c = round(math.cos(2.0 * math.pi * k / n), 15)
        s = round(sign * math.sin(2.0 * math.pi * k / n), 15)
        c = int(c) if c in (-1.0, 0.0, 1.0) else c
        s = int(s) if s in (-1.0, 0.0, 1.0) else s
        tr, ti = _cmul_const(od[k], c, s)
        out[k] = (ev[k][0] + tr, ev[k][1] + ti)
        out[k + n // 2] = (ev[k][0] - tr, ev[k][1] - ti)
    return out


def _load_complex(ref, lead, r0, cols, K):
    re = ref[lead + (pl.ds(r0, SUBLANES), cols)]
    im = ref[lead + (pl.ds(K + r0, SUBLANES), cols)]
    return re, im


def _filter_kernel(z_ref, w1_ref, b1_ref, f1_ref, w2_ref, b2_ref, f2_ref, w3_ref, delta_ref, dft_ref,
                   kf_ref, h_ref, a_ref, *, K):
    ct = kf_ref.shape[-1]
    hi = lax.Precision.HIGHEST

    @pl.when(pl.program_id(0) == 0)
    def _():
        for j in range(DFT_N2):
            z = z_ref[j]
            h = jnp.sin(f1_ref[...] * (jnp.dot(z, w1_ref[...], precision=hi, preferred_element_type=F32)
                                       + b1_ref[...]))
            h = jnp.sin(f2_ref[...] * (jnp.dot(h, w2_ref[...], precision=hi, preferred_element_type=F32)
                                       + b2_ref[...]))
            h_ref[j] = h.astype(BF16)

    inv_scale = 2.0 / (2 * DFT_N2 * K)
    for o in range(HYENA_ORDER):
        for j in range(DFT_N2):
            decay = jnp.exp(-z_ref[j, :, 0:1] * jnp.abs(delta_ref[...]))
            for d in range(2):
                hf = jnp.dot(h_ref[j], w3_ref[2 * o + d], preferred_element_type=F32) * decay
                a_ref[d, j] = jnp.dot(dft_ref[j], hf.astype(BF16), preferred_element_type=F32)

        def chunk(i, carry):
            r0 = pl.multiple_of(i * SUBLANES, SUBLANES)
            for lc in range(ct // LANES):
                cols = slice(lc * LANES, (lc + 1) * LANES)
                fw = _small_dft([_load_complex(a_ref, (0, j), r0, cols, K) for j in range(DFT_N2)], -1)
                bw = _small_dft([_load_complex(a_ref, (1, j), r0, cols, K) for j in range(DFT_N2)], -1)
                for k2 in range(DFT_N2):
                    kf_ref[o, k2, pl.ds(r0, SUBLANES), cols] = (fw[k2][0] + bw[k2][0]) * inv_scale
                    kf_ref[o, k2, pl.ds(K + r0, SUBLANES), cols] = (fw[k2][1] - bw[k2][1]) * inv_scale
            return carry

        lax.fori_loop(0, K // SUBLANES, chunk, 0)


def _filter_spectrum(S, C, fw1, fb1, ff1, fw2, fb2, ff2, fw3, dft, *, ct):
    K = S // DFT_N2
    bands = (FILTER_EMB - 1) // 2
    t = jnp.linspace(0.0, 1.0, S, dtype=F32)[:, None]
    w = 2.0 * math.pi * jnp.arange(S, dtype=F32)[:, None] / S
    fr = jnp.linspace(1e-4, bands - 1, bands, dtype=F32)[None, :]
    feats = jnp.concatenate([t, jnp.cos(fr * w), -jnp.sin(fr * w)], axis=-1)
    feats = jnp.pad(feats, ((0, 0), (0, LANES - FILTER_EMB)))
    feats = feats.reshape(K, DFT_N2, LANES).transpose(1, 0, 2)
    w1 = jnp.pad(fw1, ((0, LANES - FILTER_EMB), (0, 0)))
    w3 = fw3.reshape(FILTER_HIDDEN, 2 * HYENA_ORDER, C).transpose(1, 0, 2).astype(BF16)
    deltas = jnp.linspace(math.log(DECAY_TARGET) / SLOW_DECAY_PCT, math.log(DECAY_TARGET) / FAST_DECAY_PCT,
                          C, dtype=F32)[None, :]
    row = lambda v: v.reshape(1, -1)
    const2 = lambda shape: pl.BlockSpec(shape, lambda c: (0, 0))
    return pl.pallas_call(
        functools.partial(_filter_kernel, K=K),
        grid=(C // ct,),
        in_specs=[
            pl.BlockSpec((DFT_N2, K, LANES), lambda c: (0, 0, 0)),
            const2((LANES, FILTER_HIDDEN)), const2((1, FILTER_HIDDEN)), const2((1, FILTER_HIDDEN)),
            const2((FILTER_HIDDEN, FILTER_HIDDEN)), const2((1, FILTER_HIDDEN)), const2((1, FILTER_HIDDEN)),
            pl.BlockSpec((2 * HYENA_ORDER, FILTER_HIDDEN, ct), lambda c: (0, 0, c)),
            pl.BlockSpec((1, ct), lambda c: (0, c)),
            pl.BlockSpec((DFT_N2, 2 * K, K), lambda c: (0, 0, 0)),
        ],
        out_specs=pl.BlockSpec((HYENA_ORDER, DFT_N2, 2 * K, ct), lambda c: (0, 0, 0, c)),
        out_shape=jax.ShapeDtypeStruct((HYENA_ORDER, DFT_N2, 2 * K, C), F32),
        scratch_shapes=[pltpu.VMEM((DFT_N2, K, FILTER_HIDDEN), BF16),
                        pltpu.VMEM((2, DFT_N2, 2 * K, ct), F32)],
        compiler_params=_params("arbitrary"),
    )(feats, w1, row(fb1), row(ff1), fw2, row(fb2), row(ff2), w3, deltas, dft)


def _short_conv(p_ref, w, b):
    K = p_ref.shape[1]
    p = [p_ref[j].astype(F32) for j in range(DFT_N2)]
    rows = lax.broadcasted_iota(jnp.int32, p[0].shape, 0)
    prev_of_first = jnp.where(rows == 0, 0.0, pltpu.roll(p[DFT_N2 - 1], 1, 0))
    next_of_last = jnp.where(rows == K - 1, 0.0, pltpu.roll(p[0], K - 1, 0))
    out = []
    for j in range(DFT_N2):
        prev = p[j - 1] if j > 0 else prev_of_first
        nxt = p[j + 1] if j < DFT_N2 - 1 else next_of_last
        out.append(prev * w[0:1, :] + p[j] * w[1:2, :] + nxt * w[2:3, :] + b)
    return out


def _hyena_kernel(hv_ref, hx1_ref, hx2_ref, cw_ref, cb_ref, bias_ref, kf_ref, dft_ref, dftt_ref,
                  o_ref, sig_ref, g1_ref, g2_ref, a_ref, *, K):
    ct = o_ref.shape[-1]
    cw = cw_ref[...]
    cb = cb_ref[...]
    for idx, (src, dst) in enumerate(((hv_ref, sig_ref), (hx1_ref, g1_ref), (hx2_ref, g2_ref))):
        cols = slice(idx * ct, (idx + 1) * ct)
        for j, v in enumerate(_short_conv(src, cw[:, cols], cb[:, cols])):
            dst[j] = v

    def long_conv(order):
        for j in range(DFT_N2):
            a_ref[j] = jnp.dot(dft_ref[j], sig_ref[j].astype(BF16), preferred_element_type=F32)

        def chunk(i, carry):
            r0 = pl.multiple_of(i * SUBLANES, SUBLANES)
            for lc in range(ct // LANES):
                cols = slice(lc * LANES, (lc + 1) * LANES)
                spec = _small_dft([_load_complex(a_ref, (j,), r0, cols, K) for j in range(DFT_N2)], -1)
                prod = []
                for k2 in range(DFT_N2):
                    kr, ki = _load_complex(kf_ref, (order, k2), r0, cols, K)
                    xr, xi = spec[k2]
                    prod.append((xr * kr - xi * ki, xr * ki + xi * kr))
                back = _small_dft(prod, +1)
                for j in range(DFT_N2):
                    a_ref[j, pl.ds(r0, SUBLANES), cols] = back[j][0]
                    a_ref[j, pl.ds(K + r0, SUBLANES), cols] = back[j][1]
            return carry

        lax.fori_loop(0, K // SUBLANES, chunk, 0)
        return [jnp.dot(dftt_ref[j], a_ref[j].astype(BF16), preferred_element_type=F32) for j in range(DFT_N2)]

    bias = bias_ref[...]
    y = long_conv(0)
    for j in range(DFT_N2):
        sig_ref[j] = g1_ref[j] * (y[j] + bias[0:1, :] * sig_ref[j])
    y = long_conv(1)
    for j in range(DFT_N2):
        o_ref[j] = (g2_ref[j] * (y[j] + bias[1:2, :] * sig_ref[j])).astype(o_ref.dtype)


def _hyena_branch(hz, conv_w, conv_b, hy_bias, kf, dft, dftt, *, ct):
    B, _, K, c3 = hz.shape
    C = c3 // 3
    nct = C // ct
    cw = conv_w.reshape(3, 3, nct, ct).transpose(0, 2, 1, 3).reshape(3, nct * 3 * ct)
    cb = conv_b.reshape(3, nct, ct).transpose(1, 0, 2).reshape(1, nct * 3 * ct)
    hz_spec = lambda part: pl.BlockSpec((None, DFT_N2, K, ct), lambda c, b: (b, 0, 0, part * nct + c))
    work = pltpu.VMEM((DFT_N2, K, ct), F32)
    return pl.pallas_call(
        functools.partial(_hyena_kernel, K=K),
        grid=(nct, B),
        in_specs=[
            hz_spec(0), hz_spec(1), hz_spec(2),
            pl.BlockSpec((3, 3 * ct), lambda c, b: (0, c)),
            pl.BlockSpec((1, 3 * ct), lambda c, b: (0, c)),
            pl.BlockSpec((HYENA_ORDER, ct), lambda c, b: (0, c)),
            pl.BlockSpec((HYENA_ORDER, DFT_N2, 2 * K, ct), lambda c, b: (0, 0, 0, c)),
            pl.BlockSpec((DFT_N2, 2 * K, K), lambda c, b: (0, 0, 0), pipeline_mode=pl.Buffered(1)),
            pl.BlockSpec((DFT_N2, K, 2 * K), lambda c, b: (0, 0, 0), pipeline_mode=pl.Buffered(1)),
        ],
        out_specs=pl.BlockSpec((None, DFT_N2, K, ct), lambda c, b: (b, 0, 0, c)),
        out_shape=jax.ShapeDtypeStruct((B, DFT_N2, K, C), BF16),
        scratch_shapes=[work, work, work, pltpu.VMEM((DFT_N2, 2 * K, ct), F32)],
        compiler_params=_params("parallel", "arbitrary"),
    )(hz, hz, hz, cw, cb, hy_bias, kf, dft, dftt)


def _split3(v):
    a = v.astype(BF16)
    r = v - a.astype(F32)
    b = r.astype(BF16)
    c = (r - b.astype(F32)).astype(BF16)
    return a, b, c


def _to_natural(pt_ref, blk_ref, k):
    d, _, w = blk_ref.shape
    rows = PERM_TILE // d
    blk = blk_ref[:, k * rows:(k + 1) * rows, :].reshape(PERM_TILE, w)
    if blk.dtype == BF16:
        return jnp.dot(pt_ref[...], blk, preferred_element_type=F32)
    pieces = jnp.concatenate(_split3(blk), axis=-1)
    moved = jnp.dot(pt_ref[...], pieces, preferred_element_type=F32)
    return moved[:, :w] + moved[:, w:2 * w] + moved[:, 2 * w:]


def _merge_kernel(x_ref, gpre_ref, o0_ref, o1_ref, o2_ref, l0_ref, l1_ref, l2_ref, z_ref,
                  pt1_ref, pt2_ref, ptz_ref, expand_ref,
                  wg_ref, bg_ref, woa_ref, woh_ref, wout_ref, gpost_ref, out_ref, oatt_ref, zn_ref):
    tm, D = x_ref.shape
    for k in range(tm // PERM_TILE):
        rows = slice(k * PERM_TILE, (k + 1) * PERM_TILE)
        lses = [l0_ref[rows, :], _to_natural(pt1_ref, l1_ref, k), _to_natural(pt2_ref, l2_ref, k)]
        outs = [o0_ref[rows, :].astype(F32), _to_natural(pt1_ref, o1_ref, k), _to_natural(pt2_ref, o2_ref, k)]
        m = jnp.maximum(jnp.maximum(lses[0], lses[1]), lses[2])
        e = [jnp.exp(l - m) for l in lses]
        inv = 1.0 / (e[0] + e[1] + e[2])
        o_att = jnp.zeros((PERM_TILE, GROUP_W), F32)
        for g in range(N_GROUPS):
            alpha = e[g] * inv
            a_hi = alpha.astype(BF16)
            a_lo = (alpha - a_hi.astype(F32)).astype(BF16)
            wide = (jnp.dot(a_hi, expand_ref[...], preferred_element_type=F32)
                    + jnp.dot(a_lo, expand_ref[...], preferred_element_type=F32))
            o_att = o_att + wide * outs[g]
        oatt_ref[rows, :] = o_att.astype(BF16)
        zn_ref[rows, :] = _to_natural(ptz_ref, z_ref, k).astype(BF16)

    x = x_ref[...]
    u = _rms(x, gpre_ref[...]).astype(BF16)
    y_att = jnp.dot(oatt_ref[...], woa_ref[...], preferred_element_type=F32)
    y_hy = jnp.dot(zn_ref[...], woh_ref[...], preferred_element_type=F32)
    gate = lambda half: jax.nn.sigmoid(
        jnp.dot(u, wg_ref[:, half * D:(half + 1) * D], preferred_element_type=F32)
        + bg_ref[:, half * D:(half + 1) * D])
    merged = gate(0) * y_att + gate(1) * y_hy
    mix = jnp.dot(merged.astype(BF16), wout_ref[...], preferred_element_type=F32)
    out_ref[...] = x + _rms(mix, gpost_ref[...])


def _merge(x, gpre, o_groups, lse_groups, z, wg, bg, woa, woh, wout, gpost, *, tm):
    B, S, D = x.shape
    C = z.shape[-1]
    dils = [d for _, d in ATT_GROUPS]

    def rmajor(a, d):
        w = a.shape[-1]
        return a.reshape(B, d, S // d, w), pl.BlockSpec((None, d, tm // d, w), lambda b, i: (b, 0, i, 0))

    nat = lambda w: pl.BlockSpec((None, tm, w), lambda b, i: (b, i, 0))
    full = lambda a: pl.BlockSpec(a.shape, lambda b, i: (0,) * a.ndim, pipeline_mode=pl.Buffered(1))
    o1, o1_spec = rmajor(o_groups[1], dils[1])
    o2, o2_spec = rmajor(o_groups[2], dils[2])
    l1, l1_spec = rmajor(lse_groups[1], dils[1])
    l2, l2_spec = rmajor(lse_groups[2], dils[2])
    z_spec = pl.BlockSpec((None, DFT_N2, tm // DFT_N2, C), lambda b, i: (b, 0, i, 0))
    pts = [jnp.asarray(_perm_matrix(d).T, dtype=BF16) for d in (dils[1], dils[2], DFT_N2)]
    expand = np.zeros((HEAD_DIM, GROUP_W), np.float32)
    for h in range(HEADS):
        expand[h * LSE_LANES, h * HEAD_DIM:(h + 1) * HEAD_DIM] = 1.0
    expand = jnp.asarray(expand, dtype=BF16)
    consts = pts + [expand, wg, bg, woa, woh, wout, gpost]
    return pl.pallas_call(
        _merge_kernel,
        grid=(B, S // tm),
        in_specs=[nat(D), full(gpre), nat(GROUP_W), o1_spec, o2_spec, nat(HEAD_DIM), l1_spec, l2_spec, z_spec]
                 + [full(a) for a in consts],
        out_specs=nat(D),
        out_shape=jax.ShapeDtypeStruct((B, S, D), F32),
        scratch_shapes=[pltpu.VMEM((tm, GROUP_W), BF16), pltpu.VMEM((tm, C), BF16)],
        compiler_params=_params("parallel", "parallel"),
    )(x, gpre, o_groups[0], o1, o2, lse_groups[0], l1, l2, z, *consts)


def _ffn_kernel(x_ref, gpre_ref, wgu_ref, wdown_ref, gpost_ref, out_ref, *, n_chunks):
    x = x_ref[...]
    u = _rms(x, gpre_ref[...]).astype(BF16)
    dff = wdown_ref.shape[0]
    ck = dff // n_chunks
    acc = jnp.zeros(x.shape, F32)
    for c in range(n_chunks):
        a = jnp.dot(u, wgu_ref[:, c * ck:(c + 1) * ck], preferred_element_type=F32)
        b = jnp.dot(u, wgu_ref[:, dff + c * ck:dff + (c + 1) * ck], preferred_element_type=F32)
        h = (a * jax.nn.sigmoid(a) * b).astype(BF16)
        acc = acc + jnp.dot(h, wdown_ref[c * ck:(c + 1) * ck, :], preferred_element_type=F32)
    out_ref[...] = x + _rms(acc, gpost_ref[...])


def _ffn(x, gpre, wgu, wdown, gpost, *, tm, n_chunks):
    B, S, D = x.shape
    T = B * S
    full = lambda a: pl.BlockSpec(a.shape, lambda i: (0,) * a.ndim, pipeline_mode=pl.Buffered(1))
    out = pl.pallas_call(
        functools.partial(_ffn_kernel, n_chunks=n_chunks),
        grid=(T // tm,),
        in_specs=[pl.BlockSpec((tm, D), lambda i: (i, 0)), full(gpre), full(wgu), full(wdown), full(gpost)],
        out_specs=pl.BlockSpec((tm, D), lambda i: (i, 0)),
        out_shape=jax.ShapeDtypeStruct((T, D), F32),
        compiler_params=_params("parallel"),
    )(x.reshape(T, D), gpre, wgu, wdown, gpost)
    return out.reshape(B, S, D)


def kernel(x, norm_mix_pre, norm_mix_post, norm_ffn_pre, norm_ffn_post, w_in, conv_w, conv_b, filt_w1, filt_b1, filt_freq1, filt_w2, filt_b2, filt_freq2, filt_w3, hyena_bias, w_o_attn, w_o_hyena, w_gate, b_gate, w_out, w_gate_up, w_down):
    B, S, D = x.shape
    depth = w_in.shape[0]
    C = w_o_hyena.shape[1]
    K = S // DFT_N2
    assert S % PERM_TILE == 0 and w_in.shape[2] == 3 * ATT_W + 3 * C

    tables = _rotary_tables(S)
    dft, dftt = _dft_matrices(S)
    row = lambda v: v.reshape(1, -1)
    hy_tn = 768
    hy_blocks = list(range(3 * ATT_W // hy_tn, (3 * ATT_W + 3 * C) // hy_tn))

    for l in range(depth):
        w_in_l = w_in[l].astype(BF16)
        g_pre = row(norm_mix_pre[l])
        groups = []
        for g, (_, dil) in enumerate(ATT_GROUPS):
            qkv = _proj(x, g_pre, w_in_l, dil=dil, tn=GROUP_W, col_blocks=[g, N_GROUPS + g, 2 * N_GROUPS + g])
            groups.append(_attention_group(qkv, tables, dil))
        hz = _proj(x, g_pre, w_in_l, dil=DFT_N2, tn=hy_tn, col_blocks=hy_blocks).reshape(B, DFT_N2, K, 3 * C)

        kf = _filter_spectrum(S, C, filt_w1[l], filt_b1[l], filt_freq1[l], filt_w2[l], filt_b2[l],
                              filt_freq2[l], filt_w3[l], dft, ct=128)
        z = _hyena_branch(hz, conv_w[l], conv_b[l], hyena_bias[l], kf, dft, dftt, ct=256)

        x = _merge(x, g_pre, [o for o, _ in groups], [s for _, s in groups], z,
                   w_gate[l].astype(BF16), row(b_gate[l]), w_o_attn[l].astype(BF16),
                   w_o_hyena[l].astype(BF16), w_out[l].astype(BF16), row(norm_mix_post[l]), tm=512)
        x = _ffn(x, row(norm_ffn_pre[l]), w_gate_up[l].astype(BF16), w_down[l].astype(BF16),
                 row(norm_ffn_post[l]), tm=512, n_chunks=2)
    return x
```

```python
import functools
import math

import numpy as np
import jax
import jax.numpy as jnp
from jax import lax
from jax.experimental import pallas as pl
from jax.experimental.pallas import tpu as pltpu

F32 = jnp.float32
BF16 = jnp.bfloat16

HEAD_DIM = 128
HEADS = 4
ATT_GROUPS = ((128, 1), (512, 4), (2048, 16))
N_GROUPS = len(ATT_GROUPS)
GROUP_W = HEADS * HEAD_DIM
ATT_W = N_GROUPS * GROUP_W
ROPE_DIM = HEAD_DIM // 4
ROPE_HALF = ROPE_DIM // 2
ROPE_THETA = 500000.0
NEG_INF = -1e30
FILTER_EMB = 33
FILTER_HIDDEN = 64
HYENA_ORDER = 2
DECAY_TARGET = 1e-2
FAST_DECAY_PCT = 0.3
SLOW_DECAY_PCT = 1.5
RMS_EPS = 1e-6

LANES = 128
SUBLANES = 8
VMEM_LIMIT_BYTES = 56 * 1024 * 1024

DFT_N2 = 8
Q_BLOCK = 128
HALF_WIN = 64
PERM_TILE = 256
LSE_LANES = HEAD_DIM // HEADS
CHUNK_UNROLL = 4


def _params(*sem):
    return pltpu.CompilerParams(dimension_semantics=sem, vmem_limit_bytes=VMEM_LIMIT_BYTES)


def _rms(x, gain):
    return x * lax.rsqrt(jnp.mean(x * x, axis=-1, keepdims=True) + RMS_EPS) * gain


def _perm_matrix(dil):
    rows = PERM_TILE // dil
    p = np.zeros((PERM_TILE, PERM_TILE), np.float32)
    for r in range(dil):
        for t in range(rows):
            p[r * rows + t, dil * t + r] = 1.0
    return p


def _proj_kernel(x_ref, g_ref, p_ref, *refs, dil):
    w_refs, o_ref, u_ref = refs[:-2], refs[-2], refs[-1]
    S, D = x_ref.shape
    ls = S // dil
    rows = PERM_TILE // dil
    tw = w_refs[0].shape[1]

    @pl.when(pl.program_id(1) == 0)
    def _():
        for k in range(S // PERM_TILE):
            u = _rms(x_ref[k * PERM_TILE:(k + 1) * PERM_TILE, :], g_ref[...]).astype(BF16)
            if dil == 1:
                u_ref[k * PERM_TILE:(k + 1) * PERM_TILE, :] = u
            else:
                up = jnp.dot(p_ref[...], u, preferred_element_type=F32).astype(BF16)
                for r in range(dil):
                    u_ref[r * ls + k * rows:r * ls + (k + 1) * rows, :] = up[r * rows:(r + 1) * rows]

    for i, w_ref in enumerate(w_refs):
        res = jnp.dot(u_ref[...], w_ref[...], preferred_element_type=F32)
        o_ref[:, i * tw:(i + 1) * tw] = res.astype(o_ref.dtype)


def _proj(x, gain, w_bf16, *, dil, tw, first_blocks, steps):
    B, S, D = x.shape
    n_w = len(first_blocks)
    perm = jnp.asarray(_perm_matrix(dil), dtype=BF16)
    w_spec = lambda f: pl.BlockSpec((D, tw), lambda b, c: (0, f + c * n_w))
    return pl.pallas_call(
        functools.partial(_proj_kernel, dil=dil),
        grid=(B, steps),
        in_specs=[
            pl.BlockSpec((None, S, D), lambda b, c: (b, 0, 0)),
            pl.BlockSpec((1, D), lambda b, c: (0, 0)),
            pl.BlockSpec((PERM_TILE, PERM_TILE), lambda b, c: (0, 0)),
        ] + [w_spec(f) for f in first_blocks],
        out_specs=pl.BlockSpec((None, S, n_w * tw), lambda b, c: (b, 0, c)),
        out_shape=jax.ShapeDtypeStruct((B, S, steps * n_w * tw), BF16),
        scratch_shapes=[pltpu.VMEM((S, D), BF16)],
        compiler_params=_params("parallel", "arbitrary"),
    )(x, gain, perm, *([w_bf16] * n_w))


def _rotary_tables(S):
    pos = jnp.arange(S, dtype=jnp.int32)
    inv = ROPE_THETA ** (-jnp.arange(0, ROPE_DIM, 2, dtype=F32) / ROPE_DIM)
    ang = pos.astype(F32)[:, None] * inv[None, :]
    cos, sin = jnp.cos(ang), jnp.sin(ang)
    rest = HEAD_DIM - ROPE_DIM
    c = jnp.concatenate([cos, cos, jnp.ones((S, rest), F32)], axis=-1)
    sa = jnp.concatenate([-sin, jnp.zeros((S, HEAD_DIM - ROPE_HALF), F32)], axis=-1)
    sb = jnp.concatenate([jnp.zeros((S, ROPE_HALF), F32), sin, jnp.zeros((S, rest), F32)], axis=-1)
    return c, sa, sb


def _rotate(x, c, sa, sb):
    return x * c + pltpu.roll(x, HEAD_DIM - ROPE_HALF, 1) * sa + pltpu.roll(x, ROPE_HALF, 1) * sb


def _attn_kernel(q_ref, k_ref, v_ref, c_ref, sa_ref, sb_ref, o_ref, lse_ref, qr_ref, kr_ref, vx_ref, *, ls):
    S = q_ref.shape[0]
    kw = min(2 * Q_BLOCK, ls)
    blocks_per_seq = ls // Q_BLOCK
    nblk = S // Q_BLOCK
    scale = HEAD_DIM ** -0.5

    def prepare(n, carry):
        r0 = pl.multiple_of(n * Q_BLOCK, Q_BLOCK)
        rows = pl.ds(r0, Q_BLOCK)
        c, sa, sb = c_ref[rows, :], sa_ref[rows, :], sb_ref[rows, :]
        ones = jnp.ones((Q_BLOCK, HEAD_DIM), BF16)
        for h in range(HEADS):
            cols = slice(h * HEAD_DIM, (h + 1) * HEAD_DIM)
            qr_ref[rows, cols] = (_rotate(q_ref[rows, cols].astype(F32), c, sa, sb) * scale).astype(BF16)
            kr_ref[rows, cols] = _rotate(k_ref[rows, cols].astype(F32), c, sa, sb).astype(BF16)
            vx_ref[rows, 2 * h * HEAD_DIM:(2 * h + 1) * HEAD_DIM] = v_ref[rows, cols]
            vx_ref[rows, (2 * h + 1) * HEAD_DIM:(2 * h + 2) * HEAD_DIM] = ones
        return carry

    prepare(0, 0)
    prepare(1, 0)

    row = lax.broadcasted_iota(jnp.int32, (Q_BLOCK, kw), 0)
    col = lax.broadcasted_iota(jnp.int32, (Q_BLOCK, kw), 1)
    rel = col - row
    lane = lax.broadcasted_iota(jnp.int32, (Q_BLOCK, HEAD_DIM), 1)

    def block(n, carry):
        base = (n // blocks_per_seq) * ls
        i0 = pl.multiple_of(n * Q_BLOCK, Q_BLOCK)
        k0 = pl.multiple_of(base + jnp.clip(i0 - base - HALF_WIN, 0, ls - kw), HALF_WIN)
        d = rel + (k0 - i0)
        valid = (d <= HALF_WIN) & (d >= -HALF_WIN)
        qrows = pl.ds(i0, Q_BLOCK)
        krows = pl.ds(k0, kw)
        head_cols = [slice(h * HEAD_DIM, (h + 1) * HEAD_DIM) for h in range(HEADS)]
        scores = [lax.dot_general(qr_ref[qrows, cols], kr_ref[krows, cols], (((1,), (1,)), ((), ())),
                                  preferred_element_type=F32) for cols in head_cols]
        prepare(jnp.minimum(n + 2, nblk - 1), 0)
        maxes, probs = [], []
        for s in scores:
            s = jnp.where(valid, s, NEG_INF)
            m = jnp.max(s, axis=-1, keepdims=True)
            maxes.append(m)
            probs.append(jnp.exp(s - m).astype(BF16))
        pv = [jnp.dot(p, vx_ref[krows, 2 * h * HEAD_DIM:(2 * h + 2) * HEAD_DIM], preferred_element_type=F32)
              for h, p in enumerate(probs)]
        lse_tile = jnp.zeros((Q_BLOCK, HEAD_DIM), F32)
        for h in range(HEADS):
            l = pv[h][:, HEAD_DIM:]
            o_ref[qrows, head_cols[h]] = (pv[h][:, :HEAD_DIM] / l).astype(o_ref.dtype)
            lse_tile = jnp.where(lane // LSE_LANES == h, maxes[h] + jnp.log(l), lse_tile)
        lse_ref[qrows, :] = lse_tile
        return carry

    lax.fori_loop(0, nblk, block, 0)


def _attention_group(qkv, tables, dil):
    B, S, _ = qkv.shape
    ls = S // dil
    tabs = [t.reshape(ls, dil, HEAD_DIM).transpose(1, 0, 2).reshape(S, HEAD_DIM) for t in tables]
    part = lambda which: pl.BlockSpec((None, S, GROUP_W), lambda b: (b, 0, which))
    tab_spec = pl.BlockSpec((S, HEAD_DIM), lambda b: (0, 0))
    return pl.pallas_call(
        functools.partial(_attn_kernel, ls=ls),
        grid=(B,),
        in_specs=[part(0), part(1), part(2), tab_spec, tab_spec, tab_spec],
        out_specs=[pl.BlockSpec((None, S, GROUP_W), lambda b: (b, 0, 0)),
                   pl.BlockSpec((None, S, HEAD_DIM), lambda b: (b, 0, 0))],
        out_shape=[jax.ShapeDtypeStruct((B, S, GROUP_W), BF16),
                   jax.ShapeDtypeStruct((B, S, HEAD_DIM), F32)],
        scratch_shapes=[pltpu.VMEM((S, GROUP_W), BF16), pltpu.VMEM((S, GROUP_W), BF16),
                        pltpu.VMEM((S, 2 * GROUP_W), BF16)],
        compiler_params=_params("parallel"),
    )(qkv, qkv, qkv, *tabs)


def _dft_matrices(S):
    N = 2 * S
    K = S // DFT_N2
    k1 = np.arange(K, dtype=np.float64)[:, None] + 0.5
    mats = []
    for j in range(DFT_N2):
        n = DFT_N2 * np.arange(K, dtype=np.float64)[None, :] + j
        theta = 2.0 * np.pi * k1 * n / N
        mats.append(np.concatenate([np.cos(theta), -np.sin(theta)], axis=0))
    f = np.stack(mats).astype(np.float32)
    return jnp.asarray(f).astype(BF16), jnp.asarray(f.transpose(0, 2, 1)).astype(BF16)


def _cmul_const(v, c, s):
    re, im = v
    if (c, s) == (1, 0):
        return re, im
    if (c, s) == (-1, 0):
        return -re, -im
    if (c, s) == (0, 1):
        return -im, re
    if (c, s) == (0, -1):
        return im, -re
    if abs(abs(c) - abs(s)) < 1e-12:
        sc, ss = math.copysign(1.0, c), math.copysign(1.0, s)
        a = re - im if sc * ss > 0 else re + im
        b = re + im if sc * ss > 0 else im - re
        return a * (sc * abs(c)), b * (sc * abs(c))
    return re * c - im * s, re * s + im * c


def _small_dft(xs, sign):
    n = len(xs)
    if n == 1:
        return xs
    ev = _small_dft(xs[0::2], sign)
    od = _small_dft(xs[1::2], sign)
    out = [None] * n
    for k in range(n // 2):
        c = round(math.cos(2.0 * math.pi * k / n), 15)
        s = round(sign * math.sin(2.0 * math.pi * k / n), 15)
        c = int(c) if c in (-1.0, 0.0, 1.0) else c
        s = int(s) if s in (-1.0, 0.0, 1.0) else s
        tr, ti = _cmul_const(od[k], c, s)
        out[k] = (ev[k][0] + tr, ev[k][1] + ti)
        out[k + n // 2] = (ev[k][0] - tr, ev[k][1] - ti)
    return out


def _load_complex(ref, lead, r0, cols, K):
    re = ref[lead + (pl.ds(r0, SUBLANES), cols)]
    im = ref[lead + (pl.ds(K + r0, SUBLANES), cols)]
    return re, im


def _filter_kernel(z_ref, w1_ref, b1_ref, f1_ref, w2_ref, b2_ref, f2_ref, w3_ref, delta_ref, dft_ref,
                   kf_ref, h_ref, a_ref, *, K):
    ct = kf_ref.shape[-1]
    hi = lax.Precision.HIGHEST

    @pl.when(pl.program_id(0) == 0)
    def _():
        for j in range(DFT_N2):
            z = z_ref[j]
            h = jnp.sin(f1_ref[...] * (jnp.dot(z, w1_ref[...], precision=hi, preferred_element_type=F32)
                                       + b1_ref[...]))
            h = jnp.sin(f2_ref[...] * (jnp.dot(h, w2_ref[...], precision=hi, preferred_element_type=F32)
                                       + b2_ref[...]))
            h_ref[j] = h.astype(BF16)

    inv_scale = 2.0 / (2 * DFT_N2 * K)
    for o in range(HYENA_ORDER):
        for j in range(DFT_N2):
            decay = jnp.exp(-z_ref[j, :, 0:1] * jnp.abs(delta_ref[...]))
            for d in range(2):
                hf = jnp.dot(h_ref[j], w3_ref[2 * o + d], preferred_element_type=F32) * decay
                a_ref[d, j] = jnp.dot(dft_ref[j], hf.astype(BF16), preferred_element_type=F32)

        def chunk(i, carry):
            r0 = pl.multiple_of(i * SUBLANES, SUBLANES)
            for lc in range(ct // LANES):
                cols = slice(lc * LANES, (lc + 1) * LANES)
                fw = _small_dft([_load_complex(a_ref, (0, j), r0, cols, K) for j in range(DFT_N2)], -1)
                bw = _small_dft([_load_complex(a_ref, (1, j), r0, cols, K) for j in range(DFT_N2)], -1)
                for k2 in range(DFT_N2):
                    kf_ref[o, k2, pl.ds(r0, SUBLANES), cols] = (fw[k2][0] + bw[k2][0]) * inv_scale
                    kf_ref[o, k2, pl.ds(K + r0, SUBLANES), cols] = (fw[k2][1] - bw[k2][1]) * inv_scale
            return carry

        lax.fori_loop(0, K // SUBLANES, chunk, 0)


def _filter_spectrum(S, C, fw1, fb1, ff1, fw2, fb2, ff2, fw3, dft, *, ct):
    K = S // DFT_N2
    bands = (FILTER_EMB - 1) // 2
    t = jnp.linspace(0.0, 1.0, S, dtype=F32)[:, None]
    w = 2.0 * math.pi * jnp.arange(S, dtype=F32)[:, None] / S
    fr = jnp.linspace(1e-4, bands - 1, bands, dtype=F32)[None, :]
    feats = jnp.concatenate([t, jnp.cos(fr * w), -jnp.sin(fr * w)], axis=-1)
    feats = jnp.pad(feats, ((0, 0), (0, LANES - FILTER_EMB)))
    feats = feats.reshape(K, DFT_N2, LANES).transpose(1, 0, 2)
    w1 = jnp.pad(fw1, ((0, LANES - FILTER_EMB), (0, 0)))
    w3 = fw3.reshape(FILTER_HIDDEN, 2 * HYENA_ORDER, C).transpose(1, 0, 2).astype(BF16)
    deltas = jnp.linspace(math.log(DECAY_TARGET) / SLOW_DECAY_PCT, math.log(DECAY_TARGET) / FAST_DECAY_PCT,
                          C, dtype=F32)[None, :]
    row = lambda v: v.reshape(1, -1)
    const2 = lambda shape: pl.BlockSpec(shape, lambda c: (0, 0))
    return pl.pallas_call(
        functools.partial(_filter_kernel, K=K),
        grid=(C // ct,),
        in_specs=[
            pl.BlockSpec((DFT_N2, K, LANES), lambda c: (0, 0, 0)),
            const2((LANES, FILTER_HIDDEN)), const2((1, FILTER_HIDDEN)), const2((1, FILTER_HIDDEN)),
            const2((FILTER_HIDDEN, FILTER_HIDDEN)), const2((1, FILTER_HIDDEN)), const2((1, FILTER_HIDDEN)),
            pl.BlockSpec((2 * HYENA_ORDER, FILTER_HIDDEN, ct), lambda c: (0, 0, c)),
            pl.BlockSpec((1, ct), lambda c: (0, c)),
            pl.BlockSpec((DFT_N2, 2 * K, K), lambda c: (0, 0, 0)),
        ],
        out_specs=pl.BlockSpec((HYENA_ORDER, DFT_N2, 2 * K, ct), lambda c: (0, 0, 0, c)),
        out_shape=jax.ShapeDtypeStruct((HYENA_ORDER, DFT_N2, 2 * K, C), F32),
        scratch_shapes=[pltpu.VMEM((DFT_N2, K, FILTER_HIDDEN), BF16),
                        pltpu.VMEM((2, DFT_N2, 2 * K, ct), F32)],
        compiler_params=_params("arbitrary"),
    )(feats, w1, row(fb1), row(ff1), fw2, row(fb2), row(ff2), w3, deltas, dft)


def _short_conv(p_ref, w, b):
    K = p_ref.shape[1]
    p = [p_ref[j].astype(F32) for j in range(DFT_N2)]
    rows = lax.broadcasted_iota(jnp.int32, p[0].shape, 0)
    prev_of_first = jnp.where(rows == 0, 0.0, pltpu.roll(p[DFT_N2 - 1], 1, 0))
    next_of_last = jnp.where(rows == K - 1, 0.0, pltpu.roll(p[0], K - 1, 0))
    out = []
    for j in range(DFT_N2):
        prev = p[j - 1] if j > 0 else prev_of_first
        nxt = p[j + 1] if j < DFT_N2 - 1 else next_of_last
        out.append(prev * w[0:1, :] + p[j] * w[1:2, :] + nxt * w[2:3, :] + b)
    return out


def _hyena_kernel(hv_ref, hx1_ref, hx2_ref, cw_ref, cb_ref, bias_ref, kf_ref, dft_ref, dftt_ref,
                  o_ref, sig_ref, g1_ref, g2_ref, a_ref, *, K):
    ct = o_ref.shape[-1]
    cw = cw_ref[...]
    cb = cb_ref[...]
    for idx, (src, dst) in enumerate(((hv_ref, sig_ref), (hx1_ref, g1_ref), (hx2_ref, g2_ref))):
        cols = slice(idx * ct, (idx + 1) * ct)
        for j, v in enumerate(_short_conv(src, cw[:, cols], cb[:, cols])):
            dst[j] = v

    def long_conv(order):
        for j in range(DFT_N2):
            a_ref[j] = jnp.dot(dft_ref[j], sig_ref[j].astype(BF16), preferred_element_type=F32)

        def chunk(i, carry):
            pieces = []
            for sub in range(CHUNK_UNROLL):
                r0 = pl.multiple_of((i * CHUNK_UNROLL + sub) * SUBLANES, SUBLANES)
                for lc in range(ct // LANES):
                    pieces.append((r0, slice(lc * LANES, (lc + 1) * LANES)))
            results = []
            for r0, cols in pieces:
                spec = _small_dft([_load_complex(a_ref, (j,), r0, cols, K) for j in range(DFT_N2)], -1)
                prod = []
                for k2 in range(DFT_N2):
                    kr, ki = _load_complex(kf_ref, (order, k2), r0, cols, K)
                    xr, xi = spec[k2]
                    prod.append((xr * kr - xi * ki, xr * ki + xi * kr))
                results.append(_small_dft(prod, +1))
            for (r0, cols), back in zip(pieces, results):
                for j in range(DFT_N2):
                    a_ref[j, pl.ds(r0, SUBLANES), cols] = back[j][0]
                    a_ref[j, pl.ds(K + r0, SUBLANES), cols] = back[j][1]
            return carry

        lax.fori_loop(0, K // (SUBLANES * CHUNK_UNROLL), chunk, 0)
        return [jnp.dot(dftt_ref[j], a_ref[j].astype(BF16), preferred_element_type=F32) for j in range(DFT_N2)]

    bias = bias_ref[...]
    y = long_conv(0)
    for j in range(DFT_N2):
        sig_ref[j] = g1_ref[j] * (y[j] + bias[0:1, :] * sig_ref[j])
    y = long_conv(1)
    for j in range(DFT_N2):
        o_ref[j] = (g2_ref[j] * (y[j] + bias[1:2, :] * sig_ref[j])).astype(o_ref.dtype)


def _hyena_branch(hz, conv_w, conv_b, hy_bias, kf, dft, dftt, *, ct):
    B, _, K, c3 = hz.shape
    C = c3 // 3
    nct = C // ct
    cw = conv_w.reshape(3, 3, nct, ct).transpose(0, 2, 1, 3).reshape(3, nct * 3 * ct)
    cb = conv_b.reshape(3, nct, ct).transpose(1, 0, 2).reshape(1, nct * 3 * ct)
    hz_spec = lambda part: pl.BlockSpec((None, DFT_N2, K, ct), lambda c, b: (b, 0, 0, part * nct + c))
    work = pltpu.VMEM((DFT_N2, K, ct), F32)
    return pl.pallas_call(
        functools.partial(_hyena_kernel, K=K),
        grid=(nct, B),
        in_specs=[
            hz_spec(0), hz_spec(1), hz_spec(2),
            pl.BlockSpec((3, 3 * ct), lambda c, b: (0, c)),
            pl.BlockSpec((1, 3 * ct), lambda c, b: (0, c)),
            pl.BlockSpec((HYENA_ORDER, ct), lambda c, b: (0, c)),
            pl.BlockSpec((HYENA_ORDER, DFT_N2, 2 * K, ct), lambda c, b: (0, 0, 0, c)),
            pl.BlockSpec((DFT_N2, 2 * K, K), lambda c, b: (0, 0, 0), pipeline_mode=pl.Buffered(1)),
            pl.BlockSpec((DFT_N2, K, 2 * K), lambda c, b: (0, 0, 0), pipeline_mode=pl.Buffered(1)),
        ],
        out_specs=pl.BlockSpec((None, DFT_N2, K, ct), lambda c, b: (b, 0, 0, c)),
        out_shape=jax.ShapeDtypeStruct((B, DFT_N2, K, C), BF16),
        scratch_shapes=[work, work, work, pltpu.VMEM((DFT_N2, 2 * K, ct), F32)],
        compiler_params=_params("parallel", "arbitrary"),
    )(hz, hz, hz, cw, cb, hy_bias, kf, dft, dftt)


def _split3(v):
    a = v.astype(BF16)
    r = v - a.astype(F32)
    b = r.astype(BF16)
    c = (r - b.astype(F32)).astype(BF16)
    return a, b, c


def _to_natural(pt_ref, blk_ref, k):
    d, _, w = blk_ref.shape
    rows = PERM_TILE // d
    blk = blk_ref[:, k * rows:(k + 1) * rows, :].reshape(PERM_TILE, w)
    if blk.dtype == BF16:
        return jnp.dot(pt_ref[...], blk, preferred_element_type=F32)
    pieces = jnp.concatenate(_split3(blk), axis=-1)
    moved = jnp.dot(pt_ref[...], pieces, preferred_element_type=F32)
    return moved[:, :w] + moved[:, w:2 * w] + moved[:, 2 * w:]


def _merge_kernel(x_ref, gpre_ref, o0_ref, o1_ref, o2_ref, l0_ref, l1_ref, l2_ref, z_ref,
                  pt1_ref, pt2_ref, ptz_ref, expand_ref,
                  wg_ref, bg_ref, woa_ref, woh_ref, wout_ref, gpost_ref, out_ref, oatt_ref, zn_ref):
    tm, D = x_ref.shape
    for k in range(tm // PERM_TILE):
        rows = slice(k * PERM_TILE, (k + 1) * PERM_TILE)
        lses = [l0_ref[rows, :], _to_natural(pt1_ref, l1_ref, k), _to_natural(pt2_ref, l2_ref, k)]
        outs = [o0_ref[rows, :].astype(F32), _to_natural(pt1_ref, o1_ref, k), _to_natural(pt2_ref, o2_ref, k)]
        m = jnp.maximum(jnp.maximum(lses[0], lses[1]), lses[2])
        e = [jnp.exp(l - m) for l in lses]
        inv = 1.0 / (e[0] + e[1] + e[2])
        o_att = jnp.zeros((PERM_TILE, GROUP_W), F32)
        for g in range(N_GROUPS):
            alpha = e[g] * inv
            a_hi = alpha.astype(BF16)
            a_lo = (alpha - a_hi.astype(F32)).astype(BF16)
            wide = (jnp.dot(a_hi, expand_ref[...], preferred_element_type=F32)
                    + jnp.dot(a_lo, expand_ref[...], preferred_element_type=F32))
            o_att = o_att + wide * outs[g]
        oatt_ref[rows, :] = o_att.astype(BF16)
        zn_ref[rows, :] = _to_natural(ptz_ref, z_ref, k).astype(BF16)

    x = x_ref[...]
    u = _rms(x, gpre_ref[...]).astype(BF16)
    y_att = jnp.dot(oatt_ref[...], woa_ref[...], preferred_element_type=F32)
    y_hy = jnp.dot(zn_ref[...], woh_ref[...], preferred_element_type=F32)
    gate = lambda half: jax.nn.sigmoid(
        jnp.dot(u, wg_ref[:, half * D:(half + 1) * D], preferred_element_type=F32)
        + bg_ref[:, half * D:(half + 1) * D])
    merged = gate(0) * y_att + gate(1) * y_hy
    mix = jnp.dot(merged.astype(BF16), wout_ref[...], preferred_element_type=F32)
    out_ref[...] = x + _rms(mix, gpost_ref[...])


def _merge(x, gpre, o_groups, lse_groups, z, wg, bg, woa, woh, wout, gpost, *, tm):
    B, S, D = x.shape
    C = z.shape[-1]
    dils = [d for _, d in ATT_GROUPS]

    def rmajor(a, d):
        w = a.shape[-1]
        return a.reshape(B, d, S // d, w), pl.BlockSpec((None, d, tm // d, w), lambda b, i: (b, 0, i, 0))

    nat = lambda w: pl.BlockSpec((None, tm, w), lambda b, i: (b, i, 0))
    full = lambda a: pl.BlockSpec(a.shape, lambda b, i: (0,) * a.ndim, pipeline_mode=pl.Buffered(1))
    o1, o1_spec = rmajor(o_groups[1], dils[1])
    o2, o2_spec = rmajor(o_groups[2], dils[2])
    l1, l1_spec = rmajor(lse_groups[1], dils[1])
    l2, l2_spec = rmajor(lse_groups[2], dils[2])
    z_spec = pl.BlockSpec((None, DFT_N2, tm // DFT_N2, C), lambda b, i: (b, 0, i, 0))
    pts = [jnp.asarray(_perm_matrix(d).T, dtype=BF16) for d in (dils[1], dils[2], DFT_N2)]
    expand = np.zeros((HEAD_DIM, GROUP_W), np.float32)
    for h in range(HEADS):
        expand[h * LSE_LANES, h * HEAD_DIM:(h + 1) * HEAD_DIM] = 1.0
    expand = jnp.asarray(expand, dtype=BF16)
    consts = pts + [expand, wg, bg, woa, woh, wout, gpost]
    return pl.pallas_call(
        _merge_kernel,
        grid=(B, S // tm),
        in_specs=[nat(D), full(gpre), nat(GROUP_W), o1_spec, o2_spec, nat(HEAD_DIM), l1_spec, l2_spec, z_spec]
                 + [full(a) for a in consts],
        out_specs=nat(D),
        out_shape=jax.ShapeDtypeStruct((B, S, D), F32),
        scratch_shapes=[pltpu.VMEM((tm, GROUP_W), BF16), pltpu.VMEM((tm, C), BF16)],
        compiler_params=_params("parallel", "parallel"),
    )(x, gpre, o_groups[0], o1, o2, lse_groups[0], l1, l2, z, *consts)


def _ffn_kernel(x_ref, gpre_ref, wgu_ref, wdown_ref, gpost_ref, out_ref, *, n_chunks):
    x = x_ref[...]
    u = _rms(x, gpre_ref[...]).astype(BF16)
    dff = wdown_ref.shape[0]
    ck = dff // n_chunks
    acc = jnp.zeros(x.shape, F32)
    for c in range(n_chunks):
        a = jnp.dot(u, wgu_ref[:, c * ck:(c + 1) * ck], preferred_element_type=F32)
        b = jnp.dot(u, wgu_ref[:, dff + c * ck:dff + (c + 1) * ck], preferred_element_type=F32)
        h = (a * jax.nn.sigmoid(a) * b).astype(BF16)
        acc = acc + jnp.dot(h, wdown_ref[c * ck:(c + 1) * ck, :], preferred_element_type=F32)
    out_ref[...] = x + _rms(acc, gpost_ref[...])


def _ffn(x, gpre, wgu, wdown, gpost, *, tm, n_chunks):
    B, S, D = x.shape
    T = B * S
    full = lambda a: pl.BlockSpec(a.shape, lambda i: (0,) * a.ndim, pipeline_mode=pl.Buffered(1))
    out = pl.pallas_call(
        functools.partial(_ffn_kernel, n_chunks=n_chunks),
        grid=(T // tm,),
        in_specs=[pl.BlockSpec((tm, D), lambda i: (i, 0)), full(gpre), full(wgu), full(wdown), full(gpost)],
        out_specs=pl.BlockSpec((tm, D), lambda i: (i, 0)),
        out_shape=jax.ShapeDtypeStruct((T, D), F32),
        compiler_params=_params("parallel"),
    )(x.reshape(T, D), gpre, wgu, wdown, gpost)
    return out.reshape(B, S, D)


def kernel(x, norm_mix_pre, norm_mix_post, norm_ffn_pre, norm_ffn_post, w_in, conv_w, conv_b, filt_w1, filt_b1, filt_freq1, filt_w2, filt_b2, filt_freq2, filt_w3, hyena_bias, w_o_attn, w_o_hyena, w_gate, b_gate, w_out, w_gate_up, w_down):
    B, S, D = x.shape
    depth = w_in.shape[0]
    C = w_o_hyena.shape[1]
    K = S // DFT_N2
    assert S % PERM_TILE == 0 and w_in.shape[2] == 3 * ATT_W + 3 * C

    tables = _rotary_tables(S)
    dft, dftt = _dft_matrices(S)
    row = lambda v: v.reshape(1, -1)
    hy_first = 3 * ATT_W // GROUP_W
    hy_blocks_per_step = 3 * C // GROUP_W // 2

    for l in range(depth):
        w_in_l = w_in[l].astype(BF16)
        g_pre = row(norm_mix_pre[l])
        groups = []
        for g, (_, dil) in enumerate(ATT_GROUPS):
            qkv = _proj(x, g_pre, w_in_l, dil=dil, tw=GROUP_W, steps=1,
                        first_blocks=[g, N_GROUPS + g, 2 * N_GROUPS + g])
            groups.append(_attention_group(qkv, tables, dil))
        hz = _proj(x, g_pre, w_in_l, dil=DFT_N2, tw=GROUP_W, steps=2,
                   first_blocks=[hy_first + i for i in range(hy_blocks_per_step)])
        hz = hz.reshape(B, DFT_N2, K, 3 * C)

        kf = _filter_spectrum(S, C, filt_w1[l], filt_b1[l], filt_freq1[l], filt_w2[l], filt_b2[l],
                              filt_freq2[l], filt_w3[l], dft, ct=256)
        z = _hyena_branch(hz, conv_w[l], conv_b[l], hyena_bias[l], kf, dft, dftt, ct=256)

        x = _merge(x, g_pre, [o for o, _ in groups], [s for _, s in groups], z,
                   w_gate[l].astype(BF16), row(b_gate[l]), w_o_attn[l].astype(BF16),
                   w_o_hyena[l].astype(BF16), w_out[l].astype(BF16), row(norm_mix_post[l]), tm=512)
        x = _ffn(x, row(norm_ffn_pre[l]), w_gate_up[l].astype(BF16), w_down[l].astype(BF16),
                 row(norm_ffn_post[l]), tm=512, n_chunks=2)
    return x
```

```python
import functools
import math

import numpy as np
import jax
import jax.numpy as jnp
from jax import lax
from jax.experimental import pallas as pl
from jax.experimental.pallas import tpu as pltpu

F32 = jnp.float32
BF16 = jnp.bfloat16

HEAD_DIM = 128
HEADS = 4
ATT_GROUPS = ((128, 1), (512, 4), (2048, 16))
N_GROUPS = len(ATT_GROUPS)
GROUP_W = HEADS * HEAD_DIM
ATT_W = N_GROUPS * GROUP_W
ROPE_DIM = HEAD_DIM // 4
ROPE_HALF = ROPE_DIM // 2
ROPE_THETA = 500000.0
NEG_INF = -1e30
FILTER_EMB = 33
FILTER_HIDDEN = 64
HYENA_ORDER = 2
DECAY_TARGET = 1e-2
FAST_DECAY_PCT = 0.3
SLOW_DECAY_PCT = 1.5
RMS_EPS = 1e-6

LANES = 128
SUBLANES = 8
VMEM_LIMIT_BYTES = 56 * 1024 * 1024

DFT_N2 = 8
Q_BLOCK = 128
HALF_WIN = 64
PERM_TILE = 256
LSE_LANES = HEAD_DIM // HEADS

def _params(*sem):
    return pltpu.CompilerParams(dimension_semantics=sem, vmem_limit_bytes=VMEM_LIMIT_BYTES)


def _rms(x, gain):
    return x * lax.rsqrt(jnp.mean(x * x, axis=-1, keepdims=True) + RMS_EPS) * gain


def _perm_matrix(dil):
    rows = PERM_TILE // dil
    p = np.zeros((PERM_TILE, PERM_TILE), np.float32)
    for r in range(dil):
        for t in range(rows):
            p[r * rows + t, dil * t + r] = 1.0
    return p


def _proj_kernel(x_ref, g_ref, p_ref, *refs, dil):
    w_refs, o_ref, u_ref = refs[:-2], refs[-2], refs[-1]
    S, D = x_ref.shape
    ls = S // dil
    rows = PERM_TILE // dil
    tw = w_refs[0].shape[1]

    @pl.when(pl.program_id(1) == 0)
    def _():
        for k in range(S // PERM_TILE):
            u = _rms(x_ref[k * PERM_TILE:(k + 1) * PERM_TILE, :], g_ref[...]).astype(BF16)
            if dil == 1:
                u_ref[k * PERM_TILE:(k + 1) * PERM_TILE, :] = u
            else:
                up = jnp.dot(p_ref[...], u, preferred_element_type=F32).astype(BF16)
                for r in range(dil):
                    u_ref[r * ls + k * rows:r * ls + (k + 1) * rows, :] = up[r * rows:(r + 1) * rows]

    for i, w_ref in enumerate(w_refs):
        res = jnp.dot(u_ref[...], w_ref[...], preferred_element_type=F32)
        o_ref[:, i * tw:(i + 1) * tw] = res.astype(o_ref.dtype)


def _proj(x, gain, w_bf16, *, layer, dil, tw, first_blocks, steps):
    B, S, D = x.shape
    n_w = len(first_blocks)
    perm = jnp.asarray(_perm_matrix(dil), dtype=BF16)
    w_spec = lambda f: pl.BlockSpec((None, D, tw), lambda b, c: (layer, 0, f + c * n_w))
    return pl.pallas_call(
        functools.partial(_proj_kernel, dil=dil),
        grid=(B, steps),
        in_specs=[
            pl.BlockSpec((None, S, D), lambda b, c: (b, 0, 0)),
            pl.BlockSpec((1, D), lambda b, c: (0, 0)),
            pl.BlockSpec((PERM_TILE, PERM_TILE), lambda b, c: (0, 0)),
        ] + [w_spec(f) for f in first_blocks],
        out_specs=pl.BlockSpec((None, S, n_w * tw), lambda b, c: (b, 0, c)),
        out_shape=jax.ShapeDtypeStruct((B, S, steps * n_w * tw), BF16),
        scratch_shapes=[pltpu.VMEM((S, D), BF16)],
        compiler_params=_params("parallel", "arbitrary"),
    )(x, gain, perm, *([w_bf16] * n_w))


def _rotary_tables(S):
    pos = jnp.arange(S, dtype=jnp.int32)
    inv = ROPE_THETA ** (-jnp.arange(0, ROPE_DIM, 2, dtype=F32) / ROPE_DIM)
    ang = pos.astype(F32)[:, None] * inv[None, :]
    cos, sin = jnp.cos(ang), jnp.sin(ang)
    rest = HEAD_DIM - ROPE_DIM
    c = jnp.concatenate([cos, cos, jnp.ones((S, rest), F32)], axis=-1)
    sa = jnp.concatenate([-sin, jnp.zeros((S, HEAD_DIM - ROPE_HALF), F32)], axis=-1)
    sb = jnp.concatenate([jnp.zeros((S, ROPE_HALF), F32), sin, jnp.zeros((S, rest), F32)], axis=-1)
    return c, sa, sb


def _rotate(x, c, sa, sb):
    return x * c + pltpu.roll(x, HEAD_DIM - ROPE_HALF, 1) * sa + pltpu.roll(x, ROPE_HALF, 1) * sb


def _attn_kernel(q_ref, k_ref, v_ref, c_ref, sa_ref, sb_ref, o_ref, lse_ref, qr_ref, kr_ref, vx_ref, *, ls):
    S = q_ref.shape[0]
    kw = min(2 * Q_BLOCK, ls)
    blocks_per_seq = ls // Q_BLOCK
    nblk = S // Q_BLOCK
    scale = HEAD_DIM ** -0.5
    head_cols = [slice(h * HEAD_DIM, (h + 1) * HEAD_DIM) for h in range(HEADS)]

    def prepare(n):
        rows = slice(n * Q_BLOCK, (n + 1) * Q_BLOCK)
        c, sa, sb = c_ref[rows, :], sa_ref[rows, :], sb_ref[rows, :]
        ones = jnp.ones((Q_BLOCK, HEAD_DIM), BF16)
        for h, cols in enumerate(head_cols):
            qr_ref[rows, cols] = (_rotate(q_ref[rows, cols].astype(F32), c, sa, sb) * scale).astype(BF16)
            kr_ref[rows, cols] = _rotate(k_ref[rows, cols].astype(F32), c, sa, sb).astype(BF16)
            vx_ref[rows, 2 * h * HEAD_DIM:(2 * h + 1) * HEAD_DIM] = v_ref[rows, cols]
            vx_ref[rows, (2 * h + 1) * HEAD_DIM:(2 * h + 2) * HEAD_DIM] = ones

    row = lax.broadcasted_iota(jnp.int32, (Q_BLOCK, kw), 0)
    col = lax.broadcasted_iota(jnp.int32, (Q_BLOCK, kw), 1)
    rel = col - row
    lane = lax.broadcasted_iota(jnp.int32, (Q_BLOCK, HEAD_DIM), 1)

    def block(n):
        base = (n // blocks_per_seq) * ls
        i0 = n * Q_BLOCK
        k0 = base + min(max(i0 - base - HALF_WIN, 0), ls - kw)
        d = rel + (k0 - i0)
        valid = (d <= HALF_WIN) & (d >= -HALF_WIN)
        qrows = slice(i0, i0 + Q_BLOCK)
        krows = slice(k0, k0 + kw)
        scores = [lax.dot_general(qr_ref[qrows, cols], kr_ref[krows, cols], (((1,), (1,)), ((), ())),
                                  preferred_element_type=F32) for cols in head_cols]
        if n + 2 < nblk:
            prepare(n + 2)
        maxes, probs = [], []
        for s in scores:
            s = jnp.where(valid, s, NEG_INF)
            m = jnp.max(s, axis=-1, keepdims=True)
            maxes.append(m)
            probs.append(jnp.exp(s - m).astype(BF16))
        pv = [jnp.dot(p, vx_ref[krows, 2 * h * HEAD_DIM:(2 * h + 2) * HEAD_DIM], preferred_element_type=F32)
              for h, p in enumerate(probs)]
        lse_tile = jnp.zeros((Q_BLOCK, HEAD_DIM), F32)
        for h in range(HEADS):
            l = pv[h][:, HEAD_DIM:]
            o_ref[qrows, head_cols[h]] = (pv[h][:, :HEAD_DIM] / l).astype(o_ref.dtype)
            lse_tile = jnp.where(lane // LSE_LANES == h, maxes[h] + jnp.log(l), lse_tile)
        lse_ref[qrows, :] = lse_tile

    prepare(0)
    prepare(1)
    for n in range(nblk):
        block(n)


def _attention_group(qkv, tables, dil):
    B, S, _ = qkv.shape
    ls = S // dil
    tabs = [t.reshape(ls, dil, HEAD_DIM).transpose(1, 0, 2).reshape(S, HEAD_DIM) for t in tables]
    part = lambda which: pl.BlockSpec((None, S, GROUP_W), lambda b: (b, 0, which))
    tab_spec = pl.BlockSpec((S, HEAD_DIM), lambda b: (0, 0))
    return pl.pallas_call(
        functools.partial(_attn_kernel, ls=ls),
        grid=(B,),
        in_specs=[part(0), part(1), part(2), tab_spec, tab_spec, tab_spec],
        out_specs=[pl.BlockSpec((None, S, GROUP_W), lambda b: (b, 0, 0)),
                   pl.BlockSpec((None, S, HEAD_DIM), lambda b: (b, 0, 0))],
        out_shape=[jax.ShapeDtypeStruct((B, S, GROUP_W), BF16),
                   jax.ShapeDtypeStruct((B, S, HEAD_DIM), F32)],
        scratch_shapes=[pltpu.VMEM((S, GROUP_W), BF16), pltpu.VMEM((S, GROUP_W), BF16),
                        pltpu.VMEM((S, 2 * GROUP_W), BF16)],
        compiler_params=_params("parallel"),
    )(qkv, qkv, qkv, *tabs)


def _dft_matrices(S):
    N = 2 * S
    K = S // DFT_N2
    k1 = np.arange(K, dtype=np.float64)[:, None] + 0.5
    mats = []
    for j in range(DFT_N2):
        n = DFT_N2 * np.arange(K, dtype=np.float64)[None, :] + j
        theta = 2.0 * np.pi * k1 * n / N
        mats.append(np.concatenate([np.cos(theta), -np.sin(theta)], axis=0))
    f = np.stack(mats).astype(np.float32)
    return jnp.asarray(f).astype(BF16), jnp.asarray(f.transpose(0, 2, 1)).astype(BF16)


def _cmul_const(v, c, s):
    re, im = v
    if (c, s) == (1, 0):
        return re, im
    if (c, s) == (-1, 0):
        return -re, -im
    if (c, s) == (0, 1):
        return -im, re
    if (c, s) == (0, -1):
        return im, -re
    if abs(abs(c) - abs(s)) < 1e-12:
        sc, ss = math.copysign(1.0, c), math.copysign(1.0, s)
        a = re - im if sc * ss > 0 else re + im
        b = re + im if sc * ss > 0 else im - re
        return a * (sc * abs(c)), b * (sc * abs(c))
    return re * c - im * s, re * s + im * c


def _small_dft(xs, sign):
    n = len(xs)
    if n == 1:
        return xs
    ev = _small_dft(xs[0::2], sign)
    od = _small_dft(xs[1::2], sign)
    out = [None] * n
    for k in range(n // 2):
        c = round(math.cos(2.0 * math.pi * k / n), 15)
        s = round(sign * math.sin(2.0 * math.pi * k / n), 15)
        c = int(c) if c in (-1.0, 0.0, 1.0) else c
        s = int(s) if s in (-1.0, 0.0, 1.0) else s
        tr, ti = _cmul_const(od[k], c, s)
        out[k] = (ev[k][0] + tr, ev[k][1] + ti)
        out[k + n // 2] = (ev[k][0] - tr, ev[k][1] - ti)
    return out


def _load_complex(ref, lead, r0, cols, K):
    re = ref[lead + (pl.ds(r0, SUBLANES), cols)]
    im = ref[lead + (pl.ds(K + r0, SUBLANES), cols)]
    return re, im


def _filter_kernel(z_ref, w1_ref, b1_ref, f1_ref, w2_ref, b2_ref, f2_ref, w3_ref, delta_ref, dft_ref,
                   kf_ref, h_ref, a_ref, *, K):
    ct = kf_ref.shape[-1]
    hi = lax.Precision.HIGHEST

    @pl.when(pl.program_id(0) == 0)
    def _():
        for j in range(DFT_N2):
            z = z_ref[j]
            h = jnp.sin(f1_ref[...] * (jnp.dot(z, w1_ref[...], precision=hi, preferred_element_type=F32)
                                       + b1_ref[...]))
            h = jnp.sin(f2_ref[...] * (jnp.dot(h, w2_ref[...], precision=hi, preferred_element_type=F32)
                                       + b2_ref[...]))
            h_ref[j] = h.astype(BF16)

    inv_scale = 2.0 / (2 * DFT_N2 * K)
    for o in range(HYENA_ORDER):
        for j in range(DFT_N2):
            decay = jnp.exp(-z_ref[j, :, 0:1] * jnp.abs(delta_ref[...]))
            for d in range(2):
                hf = jnp.dot(h_ref[j], w3_ref[2 * o + d], preferred_element_type=F32) * decay
                a_ref[d, j] = jnp.dot(dft_ref[j], hf.astype(BF16), preferred_element_type=F32)

        def chunk(i, carry):
            r0 = pl.multiple_of(i * SUBLANES, SUBLANES)
            for lc in range(ct // LANES):
                cols = slice(lc * LANES, (lc + 1) * LANES)
                fw = _small_dft([_load_complex(a_ref, (0, j), r0, cols, K) for j in range(DFT_N2)], -1)
                bw = _small_dft([_load_complex(a_ref, (1, j), r0, cols, K) for j in range(DFT_N2)], -1)
                for k2 in range(DFT_N2):
                    kf_ref[o, k2, pl.ds(r0, SUBLANES), cols] = (fw[k2][0] + bw[k2][0]) * inv_scale
                    kf_ref[o, k2, pl.ds(K + r0, SUBLANES), cols] = (fw[k2][1] - bw[k2][1]) * inv_scale
            return carry

        lax.fori_loop(0, K // SUBLANES, chunk, 0)


def _filter_spectrum(S, C, fw1, fb1, ff1, fw2, fb2, ff2, fw3, dft, *, ct):
    K = S // DFT_N2
    bands = (FILTER_EMB - 1) // 2
    t = jnp.linspace(0.0, 1.0, S, dtype=F32)[:, None]
    w = 2.0 * math.pi * jnp.arange(S, dtype=F32)[:, None] / S
    fr = jnp.linspace(1e-4, bands - 1, bands, dtype=F32)[None, :]
    feats = jnp.concatenate([t, jnp.cos(fr * w), -jnp.sin(fr * w)], axis=-1)
    feats = jnp.pad(feats, ((0, 0), (0, LANES - FILTER_EMB)))
    feats = feats.reshape(K, DFT_N2, LANES).transpose(1, 0, 2)
    w1 = jnp.pad(fw1, ((0, LANES - FILTER_EMB), (0, 0)))
    w3 = fw3.reshape(FILTER_HIDDEN, 2 * HYENA_ORDER, C).transpose(1, 0, 2).astype(BF16)
    deltas = jnp.linspace(math.log(DECAY_TARGET) / SLOW_DECAY_PCT, math.log(DECAY_TARGET) / FAST_DECAY_PCT,
                          C, dtype=F32)[None, :]
    row = lambda v: v.reshape(1, -1)
    const2 = lambda shape: pl.BlockSpec(shape, lambda c: (0, 0))
    return pl.pallas_call(
        functools.partial(_filter_kernel, K=K),
        grid=(C // ct,),
        in_specs=[
            pl.BlockSpec((DFT_N2, K, LANES), lambda c: (0, 0, 0)),
            const2((LANES, FILTER_HIDDEN)), const2((1, FILTER_HIDDEN)), const2((1, FILTER_HIDDEN)),
            const2((FILTER_HIDDEN, FILTER_HIDDEN)), const2((1, FILTER_HIDDEN)), const2((1, FILTER_HIDDEN)),
            pl.BlockSpec((2 * HYENA_ORDER, FILTER_HIDDEN, ct), lambda c: (0, 0, c)),
            pl.BlockSpec((1, ct), lambda c: (0, c)),
            pl.BlockSpec((DFT_N2, 2 * K, K), lambda c: (0, 0, 0)),
        ],
        out_specs=pl.BlockSpec((HYENA_ORDER, DFT_N2, 2 * K, ct), lambda c: (0, 0, 0, c)),
        out_shape=jax.ShapeDtypeStruct((HYENA_ORDER, DFT_N2, 2 * K, C), F32),
        scratch_shapes=[pltpu.VMEM((DFT_N2, K, FILTER_HIDDEN), BF16),
                        pltpu.VMEM((2, DFT_N2, 2 * K, ct), F32)],
        compiler_params=_params("arbitrary"),
    )(feats, w1, row(fb1), row(ff1), fw2, row(fb2), row(ff2), w3, deltas, dft)


def _short_conv(p_ref, w, b):
    K = p_ref.shape[1]
    p = [p_ref[j].astype(F32) for j in range(DFT_N2)]
    rows = lax.broadcasted_iota(jnp.int32, p[0].shape, 0)
    prev_of_first = jnp.where(rows == 0, 0.0, pltpu.roll(p[DFT_N2 - 1], 1, 0))
    next_of_last = jnp.where(rows == K - 1, 0.0, pltpu.roll(p[0], K - 1, 0))
    out = []
    for j in range(DFT_N2):
        prev = p[j - 1] if j > 0 else prev_of_first
        nxt = p[j + 1] if j < DFT_N2 - 1 else next_of_last
        out.append(prev * w[0:1, :] + p[j] * w[1:2, :] + nxt * w[2:3, :] + b)
    return out


def _hyena_kernel(hv_ref, hx1_ref, hx2_ref, cw_ref, cb_ref, bias_ref, kf_ref, dft_ref, dftt_ref,
                  o_ref, sig0_ref, sig1_ref, a0_ref, a1_ref, *, K):
    n_units, _, _, ct = o_ref.shape
    assert n_units == 2
    sig_refs, a_refs = (sig0_ref, sig1_ref), (a0_ref, a1_ref)
    cw = cw_ref[...]
    cb = cb_ref[...]
    bias = bias_ref[...]
    conv = lambda src, u, idx: _short_conv(src.at[u], cw[:, idx * ct:(idx + 1) * ct], cb[:, idx * ct:(idx + 1) * ct])
    pieces_per_side_op = (K // SUBLANES) * (ct // LANES) // DFT_N2

    def fwd(u, j):
        a_refs[u][j] = jnp.dot(dft_ref[j], sig_refs[u][j].astype(BF16), preferred_element_type=F32)

    def inv(u, j):
        y = jnp.dot(dftt_ref[j], a_refs[u][j].astype(BF16), preferred_element_type=F32)
        a_refs[u][j, 0:K, :] = y

    def freq(u, order, side_work):
        a_ref = a_refs[u]
        pieces = [(r0, slice(lc * LANES, (lc + 1) * LANES))
                  for r0 in range(0, K, SUBLANES) for lc in range(ct // LANES)]
        for g in range(DFT_N2):
            side_work(g)
            group = pieces[g * pieces_per_side_op:(g + 1) * pieces_per_side_op]
            results = []
            for r0, cols in group:
                spec = _small_dft([_load_complex(a_ref, (j,), r0, cols, K) for j in range(DFT_N2)], -1)
                prod = []
                for k2 in range(DFT_N2):
                    kr, ki = _load_complex(kf_ref, (order, k2), r0, cols, K)
                    xr, xi = spec[k2]
                    prod.append((xr * kr - xi * ki, xr * ki + xi * kr))
                results.append(_small_dft(prod, +1))
            for (r0, cols), back in zip(group, results):
                for j in range(DFT_N2):
                    a_ref[j, r0:r0 + SUBLANES, cols] = back[j][0]
                    a_ref[j, K + r0:K + r0 + SUBLANES, cols] = back[j][1]

    def gate(u, order, gates, j):
        val = gates[j] * (a_refs[u][j, 0:K, :] + bias[order:order + 1, :] * sig_refs[u][j])
        if order + 1 < HYENA_ORDER:
            sig_refs[u][j] = val
        else:
            o_ref[u, j] = val.astype(o_ref.dtype)

    A, B = 0, 1
    for u in (A, B):
        for j, v in enumerate(conv(hv_ref, u, 0)):
            sig_refs[u][j] = v
    for order, gate_src in enumerate((hx1_ref, hx2_ref)):
        for j in range(DFT_N2):
            fwd(A, j)
        freq(A, order, lambda i: fwd(B, i))
        freq(B, order, lambda i: inv(A, i))
        gates = conv(gate_src, A, 1 + order)
        for j in range(DFT_N2):
            inv(B, j)
            gate(A, order, gates, j)
        gates = conv(gate_src, B, 1 + order)
        for j in range(DFT_N2):
            gate(B, order, gates, j)


def _hyena_branch(hz, conv_w, conv_b, hy_bias, kf, dft, dftt, *, ct):
    B, _, K, c3 = hz.shape
    C = c3 // 3
    nct = C // ct
    nb = 2
    cw = conv_w.reshape(3, 3, nct, ct).transpose(0, 2, 1, 3).reshape(3, nct * 3 * ct)
    cb = conv_b.reshape(3, nct, ct).transpose(1, 0, 2).reshape(1, nct * 3 * ct)
    hz_spec = lambda part: pl.BlockSpec((nb, DFT_N2, K, ct), lambda c, b: (b, 0, 0, part * nct + c))
    once = pl.Buffered(1)
    return pl.pallas_call(
        functools.partial(_hyena_kernel, K=K),
        grid=(nct, B // nb),
        in_specs=[
            hz_spec(0), hz_spec(1), hz_spec(2),
            pl.BlockSpec((3, 3 * ct), lambda c, b: (0, c)),
            pl.BlockSpec((1, 3 * ct), lambda c, b: (0, c)),
            pl.BlockSpec((HYENA_ORDER, ct), lambda c, b: (0, c)),
            pl.BlockSpec((HYENA_ORDER, DFT_N2, 2 * K, ct), lambda c, b: (0, 0, 0, c), pipeline_mode=once),
            pl.BlockSpec((DFT_N2, 2 * K, K), lambda c, b: (0, 0, 0), pipeline_mode=once),
            pl.BlockSpec((DFT_N2, K, 2 * K), lambda c, b: (0, 0, 0), pipeline_mode=once),
        ],
        out_specs=pl.BlockSpec((nb, DFT_N2, K, ct), lambda c, b: (b, 0, 0, c)),
        out_shape=jax.ShapeDtypeStruct((B, DFT_N2, K, C), BF16),
        scratch_shapes=[pltpu.VMEM((DFT_N2, K, ct), F32)] * nb + [pltpu.VMEM((DFT_N2, 2 * K, ct), F32)] * nb,
        compiler_params=_params("parallel", "arbitrary"),
    )(hz, hz, hz, cw, cb, hy_bias, kf, dft, dftt)


def _split2(v):
    a = v.astype(BF16)
    return a, (v - a.astype(F32)).astype(BF16)


def _to_natural(pt_ref, blk_ref, k):
    d, _, w = blk_ref.shape
    rows = PERM_TILE // d
    blk = blk_ref[:, k * rows:(k + 1) * rows, :].reshape(PERM_TILE, w)
    if blk.dtype == BF16:
        return jnp.dot(pt_ref[...], blk, preferred_element_type=F32)
    pieces = jnp.concatenate(_split2(blk), axis=-1)
    moved = jnp.dot(pt_ref[...], pieces, preferred_element_type=F32)
    return moved[:, :w] + moved[:, w:]


def _merge_kernel(x_ref, gpre_ref, o0_ref, o1_ref, o2_ref, l0_ref, l1_ref, l2_ref, z_ref,
                  pt1_ref, pt2_ref, ptz_ref, expand_ref,
                  wg_ref, bg_ref, woa_ref, woh_ref, wout_ref, gpost_ref, out_ref, oatt_ref, zn_ref):
    tm, D = x_ref.shape
    for k in range(tm // PERM_TILE):
        rows = slice(k * PERM_TILE, (k + 1) * PERM_TILE)
        lses = [l0_ref[rows, :], _to_natural(pt1_ref, l1_ref, k), _to_natural(pt2_ref, l2_ref, k)]
        outs = [o0_ref[rows, :].astype(F32), _to_natural(pt1_ref, o1_ref, k), _to_natural(pt2_ref, o2_ref, k)]
        m = jnp.maximum(jnp.maximum(lses[0], lses[1]), lses[2])
        e = [jnp.exp(l - m) for l in lses]
        inv = 1.0 / (e[0] + e[1] + e[2])
        o_att = jnp.zeros((PERM_TILE, GROUP_W), F32)
        for g in range(N_GROUPS):
            alpha = e[g] * inv
            wide = jnp.dot(alpha.astype(BF16), expand_ref[...], preferred_element_type=F32)
            o_att = o_att + wide * outs[g]
        oatt_ref[rows, :] = o_att.astype(BF16)
        zn_ref[rows, :] = _to_natural(ptz_ref, z_ref, k).astype(BF16)

    x = x_ref[...]
    u = _rms(x, gpre_ref[...]).astype(BF16)
    y_att = jnp.dot(oatt_ref[...], woa_ref[...], preferred_element_type=F32)
    y_hy = jnp.dot(zn_ref[...], woh_ref[...], preferred_element_type=F32)
    gate = lambda half: jax.nn.sigmoid(
        jnp.dot(u, wg_ref[:, half * D:(half + 1) * D], preferred_element_type=F32)
        + bg_ref[:, half * D:(half + 1) * D])
    merged = gate(0) * y_att + gate(1) * y_hy
    mix = jnp.dot(merged.astype(BF16), wout_ref[...], preferred_element_type=F32)
    out_ref[...] = x + _rms(mix, gpost_ref[...])


def _merge(x, gpre, o_groups, lse_groups, z, wg, bg, woa, woh, wout, gpost, *, layer, tm):
    B, S, D = x.shape
    C = z.shape[-1]
    dils = [d for _, d in ATT_GROUPS]

    def rmajor(a, d):
        w = a.shape[-1]
        return a.reshape(B, d, S // d, w), pl.BlockSpec((None, d, tm // d, w), lambda b, i: (b, 0, i, 0))

    nat = lambda w: pl.BlockSpec((None, tm, w), lambda b, i: (b, i, 0))
    full = lambda a: pl.BlockSpec(a.shape, lambda b, i: (0,) * a.ndim, pipeline_mode=pl.Buffered(1))
    stacked = lambda a: pl.BlockSpec((None,) + a.shape[1:], lambda b, i: (layer,) + (0,) * (a.ndim - 1),
                                     pipeline_mode=pl.Buffered(1))
    o1, o1_spec = rmajor(o_groups[1], dils[1])
    o2, o2_spec = rmajor(o_groups[2], dils[2])
    l1, l1_spec = rmajor(lse_groups[1], dils[1])
    l2, l2_spec = rmajor(lse_groups[2], dils[2])
    z_spec = pl.BlockSpec((None, DFT_N2, tm // DFT_N2, C), lambda b, i: (b, 0, i, 0))
    pts = [jnp.asarray(_perm_matrix(d).T, dtype=BF16) for d in (dils[1], dils[2], DFT_N2)]
    expand = np.zeros((HEAD_DIM, GROUP_W), np.float32)
    for h in range(HEADS):
        expand[h * LSE_LANES, h * HEAD_DIM:(h + 1) * HEAD_DIM] = 1.0
    expand = jnp.asarray(expand, dtype=BF16)
    consts = pts + [expand, wg, bg, woa, woh, wout, gpost]
    const_specs = [full(a) for a in pts + [expand]] + [stacked(wg), full(bg), stacked(woa), stacked(woh),
                                                       stacked(wout), full(gpost)]
    return pl.pallas_call(
        _merge_kernel,
        grid=(B, S // tm),
        in_specs=[nat(D), full(gpre), nat(GROUP_W), o1_spec, o2_spec, nat(HEAD_DIM), l1_spec, l2_spec, z_spec]
                 + const_specs,
        out_specs=nat(D),
        out_shape=jax.ShapeDtypeStruct((B, S, D), F32),
        scratch_shapes=[pltpu.VMEM((tm, GROUP_W), BF16), pltpu.VMEM((tm, C), BF16)],
        compiler_params=_params("parallel", "parallel"),
    )(x, gpre, o_groups[0], o1, o2, lse_groups[0], l1, l2, z, *consts)


def _ffn_kernel(x_ref, gpre_ref, wgu_ref, wdown_ref, gpost_ref, out_ref, *, n_chunks):
    x = x_ref[...]
    u = _rms(x, gpre_ref[...]).astype(BF16)
    dff = wdown_ref.shape[0]
    ck = dff // n_chunks
    acc = jnp.zeros(x.shape, F32)
    for c in range(n_chunks):
        a = jnp.dot(u, wgu_ref[:, c * ck:(c + 1) * ck], preferred_element_type=F32)
        b = jnp.dot(u, wgu_ref[:, dff + c * ck:dff + (c + 1) * ck], preferred_element_type=F32)
        h = (a * jax.nn.sigmoid(a) * b).astype(BF16)
        acc = acc + jnp.dot(h, wdown_ref[c * ck:(c + 1) * ck, :], preferred_element_type=F32)
    out_ref[...] = x + _rms(acc, gpost_ref[...])


def _ffn(x, gpre, wgu, wdown, gpost, *, layer, tm, n_chunks):
    B, S, D = x.shape
    T = B * S
    full = lambda a: pl.BlockSpec(a.shape, lambda i: (0,) * a.ndim, pipeline_mode=pl.Buffered(1))
    stacked = lambda a: pl.BlockSpec((None,) + a.shape[1:], lambda i: (layer,) + (0,) * (a.ndim - 1),
                                     pipeline_mode=pl.Buffered(1))
    out = pl.pallas_call(
        functools.partial(_ffn_kernel, n_chunks=n_chunks),
        grid=(T // tm,),
        in_specs=[pl.BlockSpec((tm, D), lambda i: (i, 0)), full(gpre), stacked(wgu), stacked(wdown), full(gpost)],
        out_specs=pl.BlockSpec((tm, D), lambda i: (i, 0)),
        out_shape=jax.ShapeDtypeStruct((T, D), F32),
        compiler_params=_params("parallel"),
    )(x.reshape(T, D), gpre, wgu, wdown, gpost)
    return out.reshape(B, S, D)


def kernel(x, norm_mix_pre, norm_mix_post, norm_ffn_pre, norm_ffn_post, w_in, conv_w, conv_b, filt_w1, filt_b1, filt_freq1, filt_w2, filt_b2, filt_freq2, filt_w3, hyena_bias, w_o_attn, w_o_hyena, w_gate, b_gate, w_out, w_gate_up, w_down):
    B, S, D = x.shape
    depth = w_in.shape[0]
    C = w_o_hyena.shape[1]
    K = S // DFT_N2
    assert S % PERM_TILE == 0 and w_in.shape[2] == 3 * ATT_W + 3 * C

    tables = _rotary_tables(S)
    dft, dftt = _dft_matrices(S)
    row = lambda v: v.reshape(1, -1)
    hy_first = 3 * ATT_W // GROUP_W
    hy_blocks_per_step = 3 * C // GROUP_W // 2

    w_in_b, w_gate_b, w_oa_b, w_oh_b, w_out_b, w_gu_b, w_down_b = (
        w.astype(BF16) for w in (w_in, w_gate, w_o_attn, w_o_hyena, w_out, w_gate_up, w_down))

    for l in range(depth):
        g_pre = row(norm_mix_pre[l])
        groups = []
        for g, (_, dil) in enumerate(ATT_GROUPS):
            qkv = _proj(x, g_pre, w_in_b, layer=l, dil=dil, tw=GROUP_W, steps=1,
                        first_blocks=[g, N_GROUPS + g, 2 * N_GROUPS + g])
            groups.append(_attention_group(qkv, tables, dil))
        hz = _proj(x, g_pre, w_in_b, layer=l, dil=DFT_N2, tw=GROUP_W, steps=2,
                   first_blocks=[hy_first + i for i in range(hy_blocks_per_step)])
        hz = hz.reshape(B, DFT_N2, K, 3 * C)

        kf = _filter_spectrum(S, C, filt_w1[l], filt_b1[l], filt_freq1[l], filt_w2[l], filt_b2[l],
                              filt_freq2[l], filt_w3[l], dft, ct=256)
        z = _hyena_branch(hz, conv_w[l], conv_b[l], hyena_bias[l], kf, dft, dftt, ct=256)

        x = _merge(x, g_pre, [o for o, _ in groups], [s for _, s in groups], z,
                   w_gate_b, row(b_gate[l]), w_oa_b, w_oh_b, w_out_b, row(norm_mix_post[l]), layer=l, tm=512)
        x = _ffn(x, row(norm_ffn_pre[l]), w_gu_b, w_down_b, row(norm_ffn_post[l]), layer=l, tm=512, n_chunks=2)
    return x
```

```python
import functools
import math

import numpy as np
import jax
import jax.numpy as jnp
from jax import lax
from jax.experimental import pallas as pl
from jax.experimental.pallas import tpu as pltpu

F32 = jnp.float32
BF16 = jnp.bfloat16

HEAD_DIM = 128
HEADS = 4
ATT_GROUPS = ((128, 1), (512, 4), (2048, 16))
N_GROUPS = len(ATT_GROUPS)
GROUP_W = HEADS * HEAD_DIM
ATT_W = N_GROUPS * GROUP_W
ROPE_DIM = HEAD_DIM // 4
ROPE_HALF = ROPE_DIM // 2
ROPE_THETA = 500000.0
NEG_INF = -1e30
FILTER_EMB = 33
FILTER_HIDDEN = 64
HYENA_ORDER = 2
DECAY_TARGET = 1e-2
FAST_DECAY_PCT = 0.3
SLOW_DECAY_PCT = 1.5
RMS_EPS = 1e-6

LANES = 128
SUBLANES = 8
VMEM_LIMIT_BYTES = 56 * 1024 * 1024

DFT_N2 = 8
Q_BLOCK = 128
HALF_WIN = 64
PERM_TILE = 256
LSE_LANES = HEAD_DIM // HEADS
ROTARY_ROWS = 512

def _params(*sem):
    return pltpu.CompilerParams(dimension_semantics=sem, vmem_limit_bytes=VMEM_LIMIT_BYTES)


def _rms(x, gain):
    return x * lax.rsqrt(jnp.mean(x * x, axis=-1, keepdims=True) + RMS_EPS) * gain


def _perm_matrix(dil):
    rows = PERM_TILE // dil
    p = np.zeros((PERM_TILE, PERM_TILE), np.float32)
    for r in range(dil):
        for t in range(rows):
            p[r * rows + t, dil * t + r] = 1.0
    return p


def _proj_kernel(x_ref, g_ref, p_ref, *refs, dil, rotary):
    if rotary:
        (c_ref, sa_ref, sb_ref), refs = refs[:3], refs[3:]
    w_refs, o_ref, u_ref = refs[:-2], refs[-2], refs[-1]
    S, D = x_ref.shape
    ls = S // dil
    rows = PERM_TILE // dil
    tw = w_refs[0].shape[1]

    @pl.when(pl.program_id(1) == 0)
    def _():
        for k in range(S // PERM_TILE):
            u = _rms(x_ref[k * PERM_TILE:(k + 1) * PERM_TILE, :], g_ref[...]).astype(BF16)
            if dil == 1:
                u_ref[k * PERM_TILE:(k + 1) * PERM_TILE, :] = u
            else:
                up = jnp.dot(p_ref[...], u, preferred_element_type=F32).astype(BF16)
                for r in range(dil):
                    u_ref[r * ls + k * rows:r * ls + (k + 1) * rows, :] = up[r * rows:(r + 1) * rows]

    for i, w_ref in enumerate(w_refs):
        res = jnp.dot(u_ref[...], w_ref[...], preferred_element_type=F32)
        if rotary and i < 2:
            post = HEAD_DIM ** -0.5 if i == 0 else 1.0
            for h in range(tw // HEAD_DIM):
                for r0 in range(0, S, ROTARY_ROWS):
                    rows = slice(r0, r0 + ROTARY_ROWS)
                    tile = _rotate(res[rows, h * HEAD_DIM:(h + 1) * HEAD_DIM],
                                   c_ref[rows, :], sa_ref[rows, :], sb_ref[rows, :])
                    o_ref[rows, i * tw + h * HEAD_DIM:i * tw + (h + 1) * HEAD_DIM] = (tile * post).astype(o_ref.dtype)
        else:
            o_ref[:, i * tw:(i + 1) * tw] = res.astype(o_ref.dtype)


def _proj(x, gain, w_bf16, *, layer, dil, tw, first_blocks, steps, rotary_tables=None):
    B, S, D = x.shape
    n_w = len(first_blocks)
    perm = jnp.asarray(_perm_matrix(dil), dtype=BF16)
    w_spec = lambda f: pl.BlockSpec((None, D, tw), lambda b, c: (layer, 0, f + c * n_w))
    tables = list(rotary_tables or ())
    table_spec = pl.BlockSpec((S, HEAD_DIM), lambda b, c: (0, 0), pipeline_mode=pl.Buffered(1))
    return pl.pallas_call(
        functools.partial(_proj_kernel, dil=dil, rotary=bool(tables)),
        grid=(B, steps),
        in_specs=[
            pl.BlockSpec((None, S, D), lambda b, c: (b, 0, 0)),
            pl.BlockSpec((1, D), lambda b, c: (0, 0)),
            pl.BlockSpec((PERM_TILE, PERM_TILE), lambda b, c: (0, 0)),
        ] + [table_spec] * len(tables) + [w_spec(f) for f in first_blocks],
        out_specs=pl.BlockSpec((None, S, n_w * tw), lambda b, c: (b, 0, c)),
        out_shape=jax.ShapeDtypeStruct((B, S, steps * n_w * tw), BF16),
        scratch_shapes=[pltpu.VMEM((S, D), BF16)],
        compiler_params=_params("parallel", "arbitrary"),
    )(x, gain, perm, *tables, *([w_bf16] * n_w))


def _rotary_tables(S):
    pos = jnp.arange(S, dtype=jnp.int32)
    inv = ROPE_THETA ** (-jnp.arange(0, ROPE_DIM, 2, dtype=F32) / ROPE_DIM)
    ang = pos.astype(F32)[:, None] * inv[None, :]
    cos, sin = jnp.cos(ang), jnp.sin(ang)
    rest = HEAD_DIM - ROPE_DIM
    c = jnp.concatenate([cos, cos, jnp.ones((S, rest), F32)], axis=-1)
    sa = jnp.concatenate([-sin, jnp.zeros((S, HEAD_DIM - ROPE_HALF), F32)], axis=-1)
    sb = jnp.concatenate([jnp.zeros((S, ROPE_HALF), F32), sin, jnp.zeros((S, rest), F32)], axis=-1)
    return c, sa, sb


def _rotate(x, c, sa, sb):
    return x * c + pltpu.roll(x, HEAD_DIM - ROPE_HALF, 1) * sa + pltpu.roll(x, ROPE_HALF, 1) * sb


def _attn_kernel(q_ref, k_ref, v_ref, o_ref, lse_ref, vx_ref, *, ls):
    S = q_ref.shape[0]
    kw = min(2 * Q_BLOCK, ls)
    blocks_per_seq = ls // Q_BLOCK
    nblk = S // Q_BLOCK
    head_cols = [slice(h * HEAD_DIM, (h + 1) * HEAD_DIM) for h in range(HEADS)]

    def prepare(n):
        rows = slice(n * Q_BLOCK, (n + 1) * Q_BLOCK)
        ones = jnp.ones((Q_BLOCK, HEAD_DIM), BF16)
        for h, cols in enumerate(head_cols):
            vx_ref[rows, 2 * h * HEAD_DIM:(2 * h + 1) * HEAD_DIM] = v_ref[rows, cols]
            vx_ref[rows, (2 * h + 1) * HEAD_DIM:(2 * h + 2) * HEAD_DIM] = ones

    row = lax.broadcasted_iota(jnp.int32, (Q_BLOCK, kw), 0)
    col = lax.broadcasted_iota(jnp.int32, (Q_BLOCK, kw), 1)
    rel = col - row
    lane = lax.broadcasted_iota(jnp.int32, (Q_BLOCK, HEAD_DIM), 1)

    def block(n):
        base = (n // blocks_per_seq) * ls
        i0 = n * Q_BLOCK
        k0 = base + min(max(i0 - base - HALF_WIN, 0), ls - kw)
        d = rel + (k0 - i0)
        valid = (d <= HALF_WIN) & (d >= -HALF_WIN)
        qrows = slice(i0, i0 + Q_BLOCK)
        krows = slice(k0, k0 + kw)
        scores = [lax.dot_general(q_ref[qrows, cols], k_ref[krows, cols], (((1,), (1,)), ((), ())),
                                  preferred_element_type=F32) for cols in head_cols]
        if n + 2 < nblk:
            prepare(n + 2)
        maxes, probs = [], []
        for s in scores:
            s = jnp.where(valid, s, NEG_INF)
            m = jnp.max(s, axis=-1, keepdims=True)
            maxes.append(m)
            probs.append(jnp.exp(s - m).astype(BF16))
        pv = [jnp.dot(p, vx_ref[krows, 2 * h * HEAD_DIM:(2 * h + 2) * HEAD_DIM], preferred_element_type=F32)
              for h, p in enumerate(probs)]
        lse_tile = jnp.zeros((Q_BLOCK, HEAD_DIM), F32)
        for h in range(HEADS):
            l = pv[h][:, HEAD_DIM:]
            o_ref[qrows, head_cols[h]] = (pv[h][:, :HEAD_DIM] / l).astype(o_ref.dtype)
            lse_tile = jnp.where(lane // LSE_LANES == h, maxes[h] + jnp.log(l), lse_tile)
        lse_ref[qrows, :] = lse_tile

    prepare(0)
    prepare(1)
    for n in range(nblk):
        block(n)


def _attention_group(qkv, dil):
    B, S, _ = qkv.shape
    part = lambda which: pl.BlockSpec((None, S, GROUP_W), lambda b: (b, 0, which))
    return pl.pallas_call(
        functools.partial(_attn_kernel, ls=S // dil),
        grid=(B,),
        in_specs=[part(0), part(1), part(2)],
        out_specs=[pl.BlockSpec((None, S, GROUP_W), lambda b: (b, 0, 0)),
                   pl.BlockSpec((None, S, HEAD_DIM), lambda b: (b, 0, 0))],
        out_shape=[jax.ShapeDtypeStruct((B, S, GROUP_W), BF16),
                   jax.ShapeDtypeStruct((B, S, HEAD_DIM), F32)],
        scratch_shapes=[pltpu.VMEM((S, 2 * GROUP_W), BF16)],
        compiler_params=_params("parallel"),
    )(qkv, qkv, qkv)


def _dft_matrices(S):
    N = 2 * S
    K = S // DFT_N2
    k1 = np.arange(K, dtype=np.float64)[:, None] + 0.5
    mats = []
    for j in range(DFT_N2):
        n = DFT_N2 * np.arange(K, dtype=np.float64)[None, :] + j
        theta = 2.0 * np.pi * k1 * n / N
        mats.append(np.concatenate([np.cos(theta), -np.sin(theta)], axis=0))
    f = np.stack(mats).astype(np.float32)
    return jnp.asarray(f).astype(BF16), jnp.asarray(f.transpose(0, 2, 1)).astype(BF16)


def _cmul_const(v, c, s):
    re, im = v
    if (c, s) == (1, 0):
        return re, im
    if (c, s) == (-1, 0):
        return -re, -im
    if (c, s) == (0, 1):
        return -im, re
    if (c, s) == (0, -1):
        return im, -re
    if abs(abs(c) - abs(s)) < 1e-12:
        sc, ss = math.copysign(1.0, c), math.copysign(1.0, s)
        a = re - im if sc * ss > 0 else re + im
        b = re + im if sc * ss > 0 else im - re
        return a * (sc * abs(c)), b * (sc * abs(c))
    return re * c - im * s, re * s + im * c


def _small_dft(xs, sign):
    n = len(xs)
    if n == 1:
        return xs
    ev = _small_dft(xs[0::2], sign)
    od = _small_dft(xs[1::2], sign)
    out = [None] * n
    for k in range(n // 2):
        c = round(math.cos(2.0 * math.pi * k / n), 15)
        s = round(sign * math.sin(2.0 * math.pi * k / n), 15)
        c = int(c) if c in (-1.0, 0.0, 1.0) else c
        s = int(s) if s in (-1.0, 0.0, 1.0) else s
        tr, ti = _cmul_const(od[k], c, s)
        out[k] = (ev[k][0] + tr, ev[k][1] + ti)
        out[k + n // 2] = (ev[k][0] - tr, ev[k][1] - ti)
    return out


def _load_complex(ref, lead, r0, cols, K):
    re = ref[lead + (pl.ds(r0, SUBLANES), cols)]
    im = ref[lead + (pl.ds(K + r0, SUBLANES), cols)]
    return re, im


def _filter_kernel(z_ref, w1_ref, b1_ref, f1_ref, w2_ref, b2_ref, f2_ref, w3_ref, delta_ref, dft_ref,
                   kf_ref, h_ref, a_ref, *, K):
    ct = kf_ref.shape[-1]
    hi = lax.Precision.HIGHEST

    @pl.when(pl.program_id(0) == 0)
    def _():
        for j in range(DFT_N2):
            z = z_ref[j]
            h = jnp.sin(f1_ref[...] * (jnp.dot(z, w1_ref[...], precision=hi, preferred_element_type=F32)
                                       + b1_ref[...]))
            h = jnp.sin(f2_ref[...] * (jnp.dot(h, w2_ref[...], precision=hi, preferred_element_type=F32)
                                       + b2_ref[...]))
            h_ref[j] = h.astype(BF16)

    inv_scale = 2.0 / (2 * DFT_N2 * K)
    for o in range(HYENA_ORDER):
        for j in range(DFT_N2):
            decay = jnp.exp(-z_ref[j, :, 0:1] * jnp.abs(delta_ref[...]))
            for d in range(2):
                hf = jnp.dot(h_ref[j], w3_ref[2 * o + d], preferred_element_type=F32) * decay
                a_ref[d, j] = jnp.dot(dft_ref[j], hf.astype(BF16), preferred_element_type=F32)

        def chunk(i, carry):
            r0 = pl.multiple_of(i * SUBLANES, SUBLANES)
            for lc in range(ct // LANES):
                cols = slice(lc * LANES, (lc + 1) * LANES)
                fw = _small_dft([_load_complex(a_ref, (0, j), r0, cols, K) for j in range(DFT_N2)], -1)
                bw = _small_dft([_load_complex(a_ref, (1, j), r0, cols, K) for j in range(DFT_N2)], -1)
                for k2 in range(DFT_N2):
                    kf_ref[o, k2, pl.ds(r0, SUBLANES), cols] = (fw[k2][0] + bw[k2][0]) * inv_scale
                    kf_ref[o, k2, pl.ds(K + r0, SUBLANES), cols] = (fw[k2][1] - bw[k2][1]) * inv_scale
            return carry

        lax.fori_loop(0, K // SUBLANES, chunk, 0)


def _filter_spectrum(S, C, fw1, fb1, ff1, fw2, fb2, ff2, fw3, dft, *, ct):
    K = S // DFT_N2
    bands = (FILTER_EMB - 1) // 2
    t = jnp.linspace(0.0, 1.0, S, dtype=F32)[:, None]
    w = 2.0 * math.pi * jnp.arange(S, dtype=F32)[:, None] / S
    fr = jnp.linspace(1e-4, bands - 1, bands, dtype=F32)[None, :]
    feats = jnp.concatenate([t, jnp.cos(fr * w), -jnp.sin(fr * w)], axis=-1)
    feats = jnp.pad(feats, ((0, 0), (0, LANES - FILTER_EMB)))
    feats = feats.reshape(K, DFT_N2, LANES).transpose(1, 0, 2)
    w1 = jnp.pad(fw1, ((0, LANES - FILTER_EMB), (0, 0)))
    w3 = fw3.reshape(FILTER_HIDDEN, 2 * HYENA_ORDER, C).transpose(1, 0, 2).astype(BF16)
    deltas = jnp.linspace(math.log(DECAY_TARGET) / SLOW_DECAY_PCT, math.log(DECAY_TARGET) / FAST_DECAY_PCT,
                          C, dtype=F32)[None, :]
    row = lambda v: v.reshape(1, -1)
    const2 = lambda shape: pl.BlockSpec(shape, lambda c: (0, 0))
    return pl.pallas_call(
        functools.partial(_filter_kernel, K=K),
        grid=(C // ct,),
        in_specs=[
            pl.BlockSpec((DFT_N2, K, LANES), lambda c: (0, 0, 0)),
            const2((LANES, FILTER_HIDDEN)), const2((1, FILTER_HIDDEN)), const2((1, FILTER_HIDDEN)),
            const2((FILTER_HIDDEN, FILTER_HIDDEN)), const2((1, FILTER_HIDDEN)), const2((1, FILTER_HIDDEN)),
            pl.BlockSpec((2 * HYENA_ORDER, FILTER_HIDDEN, ct), lambda c: (0, 0, c)),
            pl.BlockSpec((1, ct), lambda c: (0, c)),
            pl.BlockSpec((DFT_N2, 2 * K, K), lambda c: (0, 0, 0)),
        ],
        out_specs=pl.BlockSpec((HYENA_ORDER, DFT_N2, 2 * K, ct), lambda c: (0, 0, 0, c)),
        out_shape=jax.ShapeDtypeStruct((HYENA_ORDER, DFT_N2, 2 * K, C), F32),
        scratch_shapes=[pltpu.VMEM((DFT_N2, K, FILTER_HIDDEN), BF16),
                        pltpu.VMEM((2, DFT_N2, 2 * K, ct), F32)],
        compiler_params=_params("arbitrary"),
    )(feats, w1, row(fb1), row(ff1), fw2, row(fb2), row(ff2), w3, deltas, dft)


def _short_conv(p_ref, w, b):
    K = p_ref.shape[1]
    p = [p_ref[j].astype(F32) for j in range(DFT_N2)]
    rows = lax.broadcasted_iota(jnp.int32, p[0].shape, 0)
    prev_of_first = jnp.where(rows == 0, 0.0, pltpu.roll(p[DFT_N2 - 1], 1, 0))
    next_of_last = jnp.where(rows == K - 1, 0.0, pltpu.roll(p[0], K - 1, 0))
    out = []
    for j in range(DFT_N2):
        prev = p[j - 1] if j > 0 else prev_of_first
        nxt = p[j + 1] if j < DFT_N2 - 1 else next_of_last
        out.append(prev * w[0:1, :] + p[j] * w[1:2, :] + nxt * w[2:3, :] + b)
    return out


def _hyena_kernel(hv_ref, hx1_ref, hx2_ref, cw_ref, cb_ref, bias_ref, kf_ref, dft_ref, dftt_ref,
                  o_ref, sig0_ref, sig1_ref, a0_ref, a1_ref, *, K):
    n_units, _, _, ct = o_ref.shape
    assert n_units == 2
    sig_refs, a_refs = (sig0_ref, sig1_ref), (a0_ref, a1_ref)
    cw = cw_ref[...]
    cb = cb_ref[...]
    bias = bias_ref[...]
    conv = lambda src, u, idx: _short_conv(src.at[u], cw[:, idx * ct:(idx + 1) * ct], cb[:, idx * ct:(idx + 1) * ct])
    pieces_per_side_op = (K // SUBLANES) * (ct // LANES) // DFT_N2

    def fwd(u, j):
        a_refs[u][j] = jnp.dot(dft_ref[j], sig_refs[u][j].astype(BF16), preferred_element_type=F32)

    def inv(u, j):
        y = jnp.dot(dftt_ref[j], a_refs[u][j].astype(BF16), preferred_element_type=F32)
        a_refs[u][j, 0:K, :] = y

    def freq(u, order, side_work):
        a_ref = a_refs[u]
        pieces = [(r0, slice(lc * LANES, (lc + 1) * LANES))
                  for r0 in range(0, K, SUBLANES) for lc in range(ct // LANES)]
        for g in range(DFT_N2):
            side_work(g)
            group = pieces[g * pieces_per_side_op:(g + 1) * pieces_per_side_op]
            results = []
            for r0, cols in group:
                spec = _small_dft([_load_complex(a_ref, (j,), r0, cols, K) for j in range(DFT_N2)], -1)
                prod = []
                for k2 in range(DFT_N2):
                    kr, ki = _load_complex(kf_ref, (order, k2), r0, cols, K)
                    xr, xi = spec[k2]
                    prod.append((xr * kr - xi * ki, xr * ki + xi * kr))
                results.append(_small_dft(prod, +1))
            for (r0, cols), back in zip(group, results):
                for j in range(DFT_N2):
                    a_ref[j, r0:r0 + SUBLANES, cols] = back[j][0]
                    a_ref[j, K + r0:K + r0 + SUBLANES, cols] = back[j][1]

    def gate(u, order, gates, j):
        val = gates[j] * (a_refs[u][j, 0:K, :] + bias[order:order + 1, :] * sig_refs[u][j])
        if order + 1 < HYENA_ORDER:
            sig_refs[u][j] = val
        else:
            o_ref[u, j] = val.astype(o_ref.dtype)

    A, B = 0, 1
    for u in (A, B):
        for j, v in enumerate(conv(hv_ref, u, 0)):
            sig_refs[u][j] = v
    for order, gate_src in enumerate((hx1_ref, hx2_ref)):
        for j in range(DFT_N2):
            fwd(A, j)
        freq(A, order, lambda i: fwd(B, i))
        freq(B, order, lambda i: inv(A, i))
        gates = conv(gate_src, A, 1 + order)
        for j in range(DFT_N2):
            inv(B, j)
            gate(A, order, gates, j)
        gates = conv(gate_src, B, 1 + order)
        for j in range(DFT_N2):
            gate(B, order, gates, j)


def _hyena_branch(hz, conv_w, conv_b, hy_bias, kf, dft, dftt, *, ct):
    B, _, K, c3 = hz.shape
    C = c3 // 3
    nct = C // ct
    nb = 2
    cw = conv_w.reshape(3, 3, nct, ct).transpose(0, 2, 1, 3).reshape(3, nct * 3 * ct)
    cb = conv_b.reshape(3, nct, ct).transpose(1, 0, 2).reshape(1, nct * 3 * ct)
    hz_spec = lambda part: pl.BlockSpec((nb, DFT_N2, K, ct), lambda c, b: (b, 0, 0, part * nct + c))
    once = pl.Buffered(1)
    return pl.pallas_call(
        functools.partial(_hyena_kernel, K=K),
        grid=(nct, B // nb),
        in_specs=[
            hz_spec(0), hz_spec(1), hz_spec(2),
            pl.BlockSpec((3, 3 * ct), lambda c, b: (0, c)),
            pl.BlockSpec((1, 3 * ct), lambda c, b: (0, c)),
            pl.BlockSpec((HYENA_ORDER, ct), lambda c, b: (0, c)),
            pl.BlockSpec((HYENA_ORDER, DFT_N2, 2 * K, ct), lambda c, b: (0, 0, 0, c), pipeline_mode=once),
            pl.BlockSpec((DFT_N2, 2 * K, K), lambda c, b: (0, 0, 0), pipeline_mode=once),
            pl.BlockSpec((DFT_N2, K, 2 * K), lambda c, b: (0, 0, 0), pipeline_mode=once),
        ],
        out_specs=pl.BlockSpec((nb, DFT_N2, K, ct), lambda c, b: (b, 0, 0, c)),
        out_shape=jax.ShapeDtypeStruct((B, DFT_N2, K, C), BF16),
        scratch_shapes=[pltpu.VMEM((DFT_N2, K, ct), F32)] * nb + [pltpu.VMEM((DFT_N2, 2 * K, ct), F32)] * nb,
        compiler_params=_params("parallel", "arbitrary"),
    )(hz, hz, hz, cw, cb, hy_bias, kf, dft, dftt)


def _split2(v):
    a = v.astype(BF16)
    return a, (v - a.astype(F32)).astype(BF16)


def _to_natural(pt_ref, blk_ref, k):
    d, _, w = blk_ref.shape
    rows = PERM_TILE // d
    blk = blk_ref[:, k * rows:(k + 1) * rows, :].reshape(PERM_TILE, w)
    if blk.dtype == BF16:
        return jnp.dot(pt_ref[...], blk, preferred_element_type=F32)
    pieces = jnp.concatenate(_split2(blk), axis=-1)
    moved = jnp.dot(pt_ref[...], pieces, preferred_element_type=F32)
    return moved[:, :w] + moved[:, w:]


def _merge_kernel(x_ref, gpre_ref, o0_ref, o1_ref, o2_ref, l0_ref, l1_ref, l2_ref, z_ref,
                  pt1_ref, pt2_ref, ptz_ref, expand_ref,
                  wg_ref, bg_ref, woa_ref, woh_ref, wout_ref, gpost_ref, out_ref, oatt_ref, zn_ref):
    tm, D = x_ref.shape
    for k in range(tm // PERM_TILE):
        rows = slice(k * PERM_TILE, (k + 1) * PERM_TILE)
        lses = [l0_ref[rows, :], _to_natural(pt1_ref, l1_ref, k), _to_natural(pt2_ref, l2_ref, k)]
        outs = [o0_ref[rows, :].astype(F32), _to_natural(pt1_ref, o1_ref, k), _to_natural(pt2_ref, o2_ref, k)]
        m = jnp.maximum(jnp.maximum(lses[0], lses[1]), lses[2])
        e = [jnp.exp(l - m) for l in lses]
        inv = 1.0 / (e[0] + e[1] + e[2])
        o_att = jnp.zeros((PERM_TILE, GROUP_W), F32)
        for g in range(N_GROUPS):
            alpha = e[g] * inv
            wide = jnp.dot(alpha.astype(BF16), expand_ref[...], preferred_element_type=F32)
            o_att = o_att + wide * outs[g]
        oatt_ref[rows, :] = o_att.astype(BF16)
        zn_ref[rows, :] = _to_natural(ptz_ref, z_ref, k).astype(BF16)

    x = x_ref[...]
    u = _rms(x, gpre_ref[...]).astype(BF16)
    y_att = jnp.dot(oatt_ref[...], woa_ref[...], preferred_element_type=F32)
    y_hy = jnp.dot(zn_ref[...], woh_ref[...], preferred_element_type=F32)
    gate = lambda half: jax.nn.sigmoid(
        jnp.dot(u, wg_ref[:, half * D:(half + 1) * D], preferred_element_type=F32)
        + bg_ref[:, half * D:(half + 1) * D])
    merged = gate(0) * y_att + gate(1) * y_hy
    mix = jnp.dot(merged.astype(BF16), wout_ref[...], preferred_element_type=F32)
    out_ref[...] = x + _rms(mix, gpost_ref[...])


def _merge(x, gpre, o_groups, lse_groups, z, wg, bg, woa, woh, wout, gpost, *, layer, tm):
    B, S, D = x.shape
    C = z.shape[-1]
    dils = [d for _, d in ATT_GROUPS]

    def rmajor(a, d):
        w = a.shape[-1]
        return a.reshape(B, d, S // d, w), pl.BlockSpec((None, d, tm // d, w), lambda b, i: (b, 0, i, 0))

    nat = lambda w: pl.BlockSpec((None, tm, w), lambda b, i: (b, i, 0))
    full = lambda a: pl.BlockSpec(a.shape, lambda b, i: (0,) * a.ndim, pipeline_mode=pl.Buffered(1))
    stacked = lambda a: pl.BlockSpec((None,) + a.shape[1:], lambda b, i: (layer,) + (0,) * (a.ndim - 1),
                                     pipeline_mode=pl.Buffered(1))
    o1, o1_spec = rmajor(o_groups[1], dils[1])
    o2, o2_spec = rmajor(o_groups[2], dils[2])
    l1, l1_spec = rmajor(lse_groups[1], dils[1])
    l2, l2_spec = rmajor(lse_groups[2], dils[2])
    z_spec = pl.BlockSpec((None, DFT_N2, tm // DFT_N2, C), lambda b, i: (b, 0, i, 0))
    pts = [jnp.asarray(_perm_matrix(d).T, dtype=BF16) for d in (dils[1], dils[2], DFT_N2)]
    expand = np.zeros((HEAD_DIM, GROUP_W), np.float32)
    for h in range(HEADS):
        expand[h * LSE_LANES, h * HEAD_DIM:(h + 1) * HEAD_DIM] = 1.0
    expand = jnp.asarray(expand, dtype=BF16)
    consts = pts + [expand, wg, bg, woa, woh, wout, gpost]
    const_specs = [full(a) for a in pts + [expand]] + [stacked(wg), full(bg), stacked(woa), stacked(woh),
                                                       stacked(wout), full(gpost)]
    return pl.pallas_call(
        _merge_kernel,
        grid=(B, S // tm),
        in_specs=[nat(D), full(gpre), nat(GROUP_W), o1_spec, o2_spec, nat(HEAD_DIM), l1_spec, l2_spec, z_spec]
                 + const_specs,
        out_specs=nat(D),
        out_shape=jax.ShapeDtypeStruct((B, S, D), F32),
        scratch_shapes=[pltpu.VMEM((tm, GROUP_W), BF16), pltpu.VMEM((tm, C), BF16)],
        compiler_params=_params("parallel", "parallel"),
    )(x, gpre, o_groups[0], o1, o2, lse_groups[0], l1, l2, z, *consts)


def _ffn_kernel(x_ref, gpre_ref, wgu_ref, wdown_ref, gpost_ref, out_ref, *, n_chunks):
    x = x_ref[...]
    u = _rms(x, gpre_ref[...]).astype(BF16)
    dff = wdown_ref.shape[0]
    ck = dff // n_chunks
    acc = jnp.zeros(x.shape, F32)
    for c in range(n_chunks):
        a = jnp.dot(u, wgu_ref[:, c * ck:(c + 1) * ck], preferred_element_type=F32)
        b = jnp.dot(u, wgu_ref[:, dff + c * ck:dff + (c + 1) * ck], preferred_element_type=F32)
        h = (a * jax.nn.sigmoid(a) * b).astype(BF16)
        acc = acc + jnp.dot(h, wdown_ref[c * ck:(c + 1) * ck, :], preferred_element_type=F32)
    out_ref[...] = x + _rms(acc, gpost_ref[...])


def _ffn(x, gpre, wgu, wdown, gpost, *, layer, tm, n_chunks):
    B, S, D = x.shape
    T = B * S
    full = lambda a: pl.BlockSpec(a.shape, lambda i: (0,) * a.ndim, pipeline_mode=pl.Buffered(1))
    stacked = lambda a: pl.BlockSpec((None,) + a.shape[1:], lambda i: (layer,) + (0,) * (a.ndim - 1),
                                     pipeline_mode=pl.Buffered(1))
    out = pl.pallas_call(
        functools.partial(_ffn_kernel, n_chunks=n_chunks),
        grid=(T // tm,),
        in_specs=[pl.BlockSpec((tm, D), lambda i: (i, 0)), full(gpre), stacked(wgu), stacked(wdown), full(gpost)],
        out_specs=pl.BlockSpec((tm, D), lambda i: (i, 0)),
        out_shape=jax.ShapeDtypeStruct((T, D), F32),
        compiler_params=_params("parallel"),
    )(x.reshape(T, D), gpre, wgu, wdown, gpost)
    return out.reshape(B, S, D)


def kernel(x, norm_mix_pre, norm_mix_post, norm_ffn_pre, norm_ffn_post, w_in, conv_w, conv_b, filt_w1, filt_b1, filt_freq1, filt_w2, filt_b2, filt_freq2, filt_w3, hyena_bias, w_o_attn, w_o_hyena, w_gate, b_gate, w_out, w_gate_up, w_down):
    B, S, D = x.shape
    depth = w_in.shape[0]
    C = w_o_hyena.shape[1]
    K = S // DFT_N2
    assert S % PERM_TILE == 0 and w_in.shape[2] == 3 * ATT_W + 3 * C

    tables = {dil: [t.reshape(S // dil, dil, HEAD_DIM).transpose(1, 0, 2).reshape(S, HEAD_DIM)
                    for t in _rotary_tables(S)] for _, dil in ATT_GROUPS}
    dft, dftt = _dft_matrices(S)
    row = lambda v: v.reshape(1, -1)
    hy_first = 3 * ATT_W // GROUP_W
    hy_blocks_per_step = 3 * C // GROUP_W // 2

    w_in_b, w_gate_b, w_oa_b, w_oh_b, w_out_b, w_gu_b, w_down_b = (
        w.astype(BF16) for w in (w_in, w_gate, w_o_attn, w_o_hyena, w_out, w_gate_up, w_down))

    for l in range(depth):
        g_pre = row(norm_mix_pre[l])
        groups = []
        for g, (_, dil) in enumerate(ATT_GROUPS):
            qkv = _proj(x, g_pre, w_in_b, layer=l, dil=dil, tw=GROUP_W, steps=1,
                        first_blocks=[g, N_GROUPS + g, 2 * N_GROUPS + g], rotary_tables=tables[dil])
            groups.append(_attention_group(qkv, dil))
        hz = _proj(x, g_pre, w_in_b, layer=l, dil=DFT_N2, tw=GROUP_W, steps=2,
                   first_blocks=[hy_first + i for i in range(hy_blocks_per_step)])
        hz = hz.reshape(B, DFT_N2, K, 3 * C)

        kf = _filter_spectrum(S, C, filt_w1[l], filt_b1[l], filt_freq1[l], filt_w2[l], filt_b2[l],
                              filt_freq2[l], filt_w3[l], dft, ct=256)
        z = _hyena_branch(hz, conv_w[l], conv_b[l], hyena_bias[l], kf, dft, dftt, ct=256)

        x = _merge(x, g_pre, [o for o, _ in groups], [s for _, s in groups], z,
                   w_gate_b, row(b_gate[l]), w_oa_b, w_oh_b, w_out_b, row(norm_mix_post[l]), layer=l, tm=512)
        x = _ffn(x, row(norm_ffn_pre[l]), w_gu_b, w_down_b, row(norm_ffn_post[l]), layer=l, tm=512, n_chunks=2)
    return x
```

```python
import functools
import math

import numpy as np
import jax
import jax.numpy as jnp
from jax import lax
from jax.experimental import pallas as pl
from jax.experimental.pallas import tpu as pltpu

F32 = jnp.float32
BF16 = jnp.bfloat16

HEAD_DIM = 128
HEADS = 4
ATT_GROUPS = ((128, 1), (512, 4), (2048, 16))
N_GROUPS = len(ATT_GROUPS)
GROUP_W = HEADS * HEAD_DIM
ATT_W = N_GROUPS * GROUP_W
ROPE_DIM = HEAD_DIM // 4
ROPE_HALF = ROPE_DIM // 2
ROPE_THETA = 500000.0
NEG_INF = -1e30
FILTER_EMB = 33
FILTER_HIDDEN = 64
HYENA_ORDER = 2
DECAY_TARGET = 1e-2
FAST_DECAY_PCT = 0.3
SLOW_DECAY_PCT = 1.5
RMS_EPS = 1e-6

LANES = 128
SUBLANES = 8
VMEM_LIMIT_BYTES = 56 * 1024 * 1024

DFT_N2 = 8
FREQ_GROUP = 8
Q_BLOCK = 128
HALF_WIN = 64
PERM_TILE = 256
LSE_LANES = HEAD_DIM // HEADS
ROTARY_ROWS = 512

def _params(*sem):
    return pltpu.CompilerParams(dimension_semantics=sem, vmem_limit_bytes=VMEM_LIMIT_BYTES)


def _rms(x, gain):
    return x * lax.rsqrt(jnp.mean(x * x, axis=-1, keepdims=True) + RMS_EPS) * gain


def _perm_matrix(dil):
    rows = PERM_TILE // dil
    p = np.zeros((PERM_TILE, PERM_TILE), np.float32)
    for r in range(dil):
        for t in range(rows):
            p[r * rows + t, dil * t + r] = 1.0
    return p


def _proj_kernel(x_ref, g_ref, p_ref, *refs, dil, rotary):
    if rotary:
        (c_ref, sa_ref, sb_ref), refs = refs[:3], refs[3:]
    w_refs, o_ref, u_ref = refs[:-2], refs[-2], refs[-1]
    S, D = x_ref.shape
    ls = S // dil
    rows = PERM_TILE // dil
    tw = w_refs[0].shape[1]

    @pl.when(pl.program_id(1) == 0)
    def _():
        for k in range(S // PERM_TILE):
            u = _rms(x_ref[k * PERM_TILE:(k + 1) * PERM_TILE, :], g_ref[...]).astype(BF16)
            if dil == 1:
                u_ref[k * PERM_TILE:(k + 1) * PERM_TILE, :] = u
            else:
                up = jnp.dot(p_ref[...], u, preferred_element_type=F32).astype(BF16)
                for r in range(dil):
                    u_ref[r * ls + k * rows:r * ls + (k + 1) * rows, :] = up[r * rows:(r + 1) * rows]

    for i, w_ref in enumerate(w_refs):
        res = jnp.dot(u_ref[...], w_ref[...].astype(BF16), preferred_element_type=F32)
        if rotary and i < 2:
            post = HEAD_DIM ** -0.5 if i == 0 else 1.0
            for h in range(tw // HEAD_DIM):
                for r0 in range(0, S, ROTARY_ROWS):
                    rows = slice(r0, r0 + ROTARY_ROWS)
                    tile = _rotate(res[rows, h * HEAD_DIM:(h + 1) * HEAD_DIM],
                                   c_ref[rows, :], sa_ref[rows, :], sb_ref[rows, :])
                    o_ref[rows, i * tw + h * HEAD_DIM:i * tw + (h + 1) * HEAD_DIM] = (tile * post).astype(o_ref.dtype)
        else:
            o_ref[:, i * tw:(i + 1) * tw] = res.astype(o_ref.dtype)


def _proj(x, gain, w_bf16, *, layer, dil, tw, first_blocks, steps, rotary_tables=None):
    B, S, D = x.shape
    n_w = len(first_blocks)
    perm = jnp.asarray(_perm_matrix(dil), dtype=BF16)
    w_spec = lambda f: pl.BlockSpec((None, D, tw), lambda b, c: (layer, 0, f + c * n_w))
    tables = list(rotary_tables or ())
    table_spec = pl.BlockSpec((S, HEAD_DIM), lambda b, c: (0, 0), pipeline_mode=pl.Buffered(1))
    return pl.pallas_call(
        functools.partial(_proj_kernel, dil=dil, rotary=bool(tables)),
        grid=(B, steps),
        in_specs=[
            pl.BlockSpec((None, S, D), lambda b, c: (b, 0, 0)),
            pl.BlockSpec((1, D), lambda b, c: (0, 0)),
            pl.BlockSpec((PERM_TILE, PERM_TILE), lambda b, c: (0, 0)),
        ] + [table_spec] * len(tables) + [w_spec(f) for f in first_blocks],
        out_specs=pl.BlockSpec((None, S, n_w * tw), lambda b, c: (b, 0, c)),
        out_shape=jax.ShapeDtypeStruct((B, S, steps * n_w * tw), BF16),
        scratch_shapes=[pltpu.VMEM((S, D), BF16)],
        compiler_params=_params("parallel", "arbitrary"),
    )(x, gain, perm, *tables, *([w_bf16] * n_w))


def _rotary_tables(S):
    pos = jnp.arange(S, dtype=jnp.int32)
    inv = ROPE_THETA ** (-jnp.arange(0, ROPE_DIM, 2, dtype=F32) / ROPE_DIM)
    ang = pos.astype(F32)[:, None] * inv[None, :]
    cos, sin = jnp.cos(ang), jnp.sin(ang)
    rest = HEAD_DIM - ROPE_DIM
    c = jnp.concatenate([cos, cos, jnp.ones((S, rest), F32)], axis=-1)
    sa = jnp.concatenate([-sin, jnp.zeros((S, HEAD_DIM - ROPE_HALF), F32)], axis=-1)
    sb = jnp.concatenate([jnp.zeros((S, ROPE_HALF), F32), sin, jnp.zeros((S, rest), F32)], axis=-1)
    return c, sa, sb


def _rotate(x, c, sa, sb):
    return x * c + pltpu.roll(x, HEAD_DIM - ROPE_HALF, 1) * sa + pltpu.roll(x, ROPE_HALF, 1) * sb


def _attn_kernel(q_ref, k_ref, v_ref, o_ref, lse_ref, vx_ref, *, ls):
    S = q_ref.shape[0]
    kw = min(2 * Q_BLOCK, ls)
    blocks_per_seq = ls // Q_BLOCK
    nblk = S // Q_BLOCK
    head_cols = [slice(h * HEAD_DIM, (h + 1) * HEAD_DIM) for h in range(HEADS)]

    def prepare(n):
        rows = slice(n * Q_BLOCK, (n + 1) * Q_BLOCK)
        ones = jnp.ones((Q_BLOCK, HEAD_DIM), BF16)
        for h, cols in enumerate(head_cols):
            vx_ref[rows, 2 * h * HEAD_DIM:(2 * h + 1) * HEAD_DIM] = v_ref[rows, cols]
            vx_ref[rows, (2 * h + 1) * HEAD_DIM:(2 * h + 2) * HEAD_DIM] = ones

    row = lax.broadcasted_iota(jnp.int32, (Q_BLOCK, kw), 0)
    col = lax.broadcasted_iota(jnp.int32, (Q_BLOCK, kw), 1)
    rel = col - row
    lane = lax.broadcasted_iota(jnp.int32, (Q_BLOCK, HEAD_DIM), 1)

    def block(n):
        base = (n // blocks_per_seq) * ls
        i0 = n * Q_BLOCK
        k0 = base + min(max(i0 - base - HALF_WIN, 0), ls - kw)
        d = rel + (k0 - i0)
        valid = (d <= HALF_WIN) & (d >= -HALF_WIN)
        qrows = slice(i0, i0 + Q_BLOCK)
        krows = slice(k0, k0 + kw)
        scores = [lax.dot_general(q_ref[qrows, cols], k_ref[krows, cols], (((1,), (1,)), ((), ())),
                                  preferred_element_type=F32) for cols in head_cols]
        if n + 2 < nblk:
            prepare(n + 2)
        maxes, probs = [], []
        for s in scores:
            s = jnp.where(valid, s, NEG_INF)
            m = jnp.max(s, axis=-1, keepdims=True)
            maxes.append(m)
            probs.append(jnp.exp(s - m).astype(BF16))
        pv = [jnp.dot(p, vx_ref[krows, 2 * h * HEAD_DIM:(2 * h + 2) * HEAD_DIM], preferred_element_type=F32)
              for h, p in enumerate(probs)]
        lse_tile = jnp.zeros((Q_BLOCK, HEAD_DIM), F32)
        for h in range(HEADS):
            l = pv[h][:, HEAD_DIM:]
            o_ref[qrows, head_cols[h]] = (pv[h][:, :HEAD_DIM] / l).astype(o_ref.dtype)
            lse_tile = jnp.where(lane // LSE_LANES == h, maxes[h] + jnp.log(l), lse_tile)
        lse_ref[qrows, :] = lse_tile

    prepare(0)
    prepare(1)
    for n in range(nblk):
        block(n)


def _attention_group(qkv, dil):
    B, S, _ = qkv.shape
    part = lambda which: pl.BlockSpec((None, S, GROUP_W), lambda b: (b, 0, which))
    return pl.pallas_call(
        functools.partial(_attn_kernel, ls=S // dil),
        grid=(B,),
        in_specs=[part(0), part(1), part(2)],
        out_specs=[pl.BlockSpec((None, S, GROUP_W), lambda b: (b, 0, 0)),
                   pl.BlockSpec((None, S, HEAD_DIM), lambda b: (b, 0, 0))],
        out_shape=[jax.ShapeDtypeStruct((B, S, GROUP_W), BF16),
                   jax.ShapeDtypeStruct((B, S, HEAD_DIM), F32)],
        scratch_shapes=[pltpu.VMEM((S, 2 * GROUP_W), BF16)],
        compiler_params=_params("parallel"),
    )(qkv, qkv, qkv)


def _dft_matrices(S):
    N = 2 * S
    K = S // DFT_N2
    k1 = np.arange(K, dtype=np.float64)[:, None] + 0.5
    mats = []
    for j in range(DFT_N2):
        n = DFT_N2 * np.arange(K, dtype=np.float64)[None, :] + j
        theta = 2.0 * np.pi * k1 * n / N
        mats.append(np.concatenate([np.cos(theta), -np.sin(theta)], axis=0))
    f = np.stack(mats).astype(np.float32)
    return jnp.asarray(f).astype(BF16), jnp.asarray(f.transpose(0, 2, 1)).astype(BF16)


def _cmul_const(v, c, s):
    re, im = v
    if (c, s) == (1, 0):
        return re, im
    if (c, s) == (-1, 0):
        return -re, -im
    if (c, s) == (0, 1):
        return -im, re
    if (c, s) == (0, -1):
        return im, -re
    if abs(abs(c) - abs(s)) < 1e-12:
        sc, ss = math.copysign(1.0, c), math.copysign(1.0, s)
        a = re - im if sc * ss > 0 else re + im
        b = re + im if sc * ss > 0 else im - re
        return a * (sc * abs(c)), b * (sc * abs(c))
    return re * c - im * s, re * s + im * c


def _small_dft(xs, sign):
    n = len(xs)
    if n == 1:
        return xs
    ev = _small_dft(xs[0::2], sign)
    od = _small_dft(xs[1::2], sign)
    out = [None] * n
    for k in range(n // 2):
        c = round(math.cos(2.0 * math.pi * k / n), 15)
        s = round(sign * math.sin(2.0 * math.pi * k / n), 15)
        c = int(c) if c in (-1.0, 0.0, 1.0) else c
        s = int(s) if s in (-1.0, 0.0, 1.0) else s
        tr, ti = _cmul_const(od[k], c, s)
        out[k] = (ev[k][0] + tr, ev[k][1] + ti)
        out[k + n // 2] = (ev[k][0] - tr, ev[k][1] - ti)
    return out


def _load_complex(ref, lead, r0, cols, K):
    re = ref[lead + (pl.ds(r0, SUBLANES), cols)]
    im = ref[lead + (pl.ds(K + r0, SUBLANES), cols)]
    return re, im


def _filter_kernel(z_ref, w1_ref, b1_ref, f1_ref, w2_ref, b2_ref, f2_ref, w3_ref, delta_ref, dft_ref,
                   kf_ref, h_ref, a_ref, *, K):
    ct = kf_ref.shape[-1]
    hi = lax.Precision.HIGHEST

    @pl.when(pl.program_id(0) == 0)
    def _():
        for j in range(DFT_N2):
            z = z_ref[j]
            h = jnp.sin(f1_ref[...] * (jnp.dot(z, w1_ref[...], precision=hi, preferred_element_type=F32)
                                       + b1_ref[...]))
            h = jnp.sin(f2_ref[...] * (jnp.dot(h, w2_ref[...], precision=hi, preferred_element_type=F32)
                                       + b2_ref[...]))
            h_ref[j] = h.astype(BF16)

    inv_scale = 2.0 / (2 * DFT_N2 * K)
    for o in range(HYENA_ORDER):
        for j in range(DFT_N2):
            decay = jnp.exp(-z_ref[j, :, 0:1] * jnp.abs(delta_ref[...]))
            for d in range(2):
                hf = jnp.dot(h_ref[j], w3_ref[2 * o + d], preferred_element_type=F32) * decay
                a_ref[d, j] = jnp.dot(dft_ref[j], hf.astype(BF16), preferred_element_type=F32)

        def chunk(i, carry):
            r0 = pl.multiple_of(i * SUBLANES, SUBLANES)
            for lc in range(ct // LANES):
                cols = slice(lc * LANES, (lc + 1) * LANES)
                fw = _small_dft([_load_complex(a_ref, (0, j), r0, cols, K) for j in range(DFT_N2)], -1)
                bw = _small_dft([_load_complex(a_ref, (1, j), r0, cols, K) for j in range(DFT_N2)], -1)
                for k2 in range(DFT_N2):
                    kf_ref[o, k2, pl.ds(r0, SUBLANES), cols] = (fw[k2][0] + bw[k2][0]) * inv_scale
                    kf_ref[o, k2, pl.ds(K + r0, SUBLANES), cols] = (fw[k2][1] - bw[k2][1]) * inv_scale
            return carry

        lax.fori_loop(0, K // SUBLANES, chunk, 0)


def _filter_spectrum(S, C, fw1, fb1, ff1, fw2, fb2, ff2, fw3, dft, *, ct):
    K = S // DFT_N2
    bands = (FILTER_EMB - 1) // 2
    t = jnp.linspace(0.0, 1.0, S, dtype=F32)[:, None]
    w = 2.0 * math.pi * jnp.arange(S, dtype=F32)[:, None] / S
    fr = jnp.linspace(1e-4, bands - 1, bands, dtype=F32)[None, :]
    feats = jnp.concatenate([t, jnp.cos(fr * w), -jnp.sin(fr * w)], axis=-1)
    feats = jnp.pad(feats, ((0, 0), (0, LANES - FILTER_EMB)))
    feats = feats.reshape(K, DFT_N2, LANES).transpose(1, 0, 2)
    w1 = jnp.pad(fw1, ((0, LANES - FILTER_EMB), (0, 0)))
    w3 = fw3.reshape(FILTER_HIDDEN, 2 * HYENA_ORDER, C).transpose(1, 0, 2).astype(BF16)
    deltas = jnp.linspace(math.log(DECAY_TARGET) / SLOW_DECAY_PCT, math.log(DECAY_TARGET) / FAST_DECAY_PCT,
                          C, dtype=F32)[None, :]
    row = lambda v: v.reshape(1, -1)
    const2 = lambda shape: pl.BlockSpec(shape, lambda c: (0, 0))
    return pl.pallas_call(
        functools.partial(_filter_kernel, K=K),
        grid=(C // ct,),
        in_specs=[
            pl.BlockSpec((DFT_N2, K, LANES), lambda c: (0, 0, 0)),
            const2((LANES, FILTER_HIDDEN)), const2((1, FILTER_HIDDEN)), const2((1, FILTER_HIDDEN)),
            const2((FILTER_HIDDEN, FILTER_HIDDEN)), const2((1, FILTER_HIDDEN)), const2((1, FILTER_HIDDEN)),
            pl.BlockSpec((2 * HYENA_ORDER, FILTER_HIDDEN, ct), lambda c: (0, 0, c)),
            pl.BlockSpec((1, ct), lambda c: (0, c)),
            pl.BlockSpec((DFT_N2, 2 * K, K), lambda c: (0, 0, 0)),
        ],
        out_specs=pl.BlockSpec((HYENA_ORDER, DFT_N2, 2 * K, ct), lambda c: (0, 0, 0, c)),
        out_shape=jax.ShapeDtypeStruct((HYENA_ORDER, DFT_N2, 2 * K, C), F32),
        scratch_shapes=[pltpu.VMEM((DFT_N2, K, FILTER_HIDDEN), BF16),
                        pltpu.VMEM((2, DFT_N2, 2 * K, ct), F32)],
        compiler_params=_params("arbitrary"),
    )(feats, w1, row(fb1), row(ff1), fw2, row(fb2), row(ff2), w3, deltas, dft)


def _short_conv(p_ref, w, b):
    K = p_ref.shape[1]
    p = [p_ref[j].astype(F32) for j in range(DFT_N2)]
    rows = lax.broadcasted_iota(jnp.int32, p[0].shape, 0)
    prev_of_first = jnp.where(rows == 0, 0.0, pltpu.roll(p[DFT_N2 - 1], 1, 0))
    next_of_last = jnp.where(rows == K - 1, 0.0, pltpu.roll(p[0], K - 1, 0))
    out = []
    for j in range(DFT_N2):
        prev = p[j - 1] if j > 0 else prev_of_first
        nxt = p[j + 1] if j < DFT_N2 - 1 else next_of_last
        out.append(prev * w[0:1, :] + p[j] * w[1:2, :] + nxt * w[2:3, :] + b)
    return out


def _hyena_kernel(hv_ref, hx1_ref, hx2_ref, cw_ref, cb_ref, bias_ref, kf_ref, dft_ref, dftt_ref,
                  o_ref, sig0_ref, sig1_ref, a0_ref, a1_ref, *, K):
    n_units, _, _, ct = o_ref.shape
    assert n_units == 2
    sig_refs, a_refs = (sig0_ref, sig1_ref), (a0_ref, a1_ref)
    cw = cw_ref[...]
    cb = cb_ref[...]
    bias = bias_ref[...]
    conv = lambda src, u, idx: _short_conv(src.at[u], cw[:, idx * ct:(idx + 1) * ct], cb[:, idx * ct:(idx + 1) * ct])
    groups_per_side_op = (K // SUBLANES) * (ct // LANES) // FREQ_GROUP // DFT_N2

    def fwd(u, j):
        a_refs[u][j] = jnp.dot(dft_ref[j], sig_refs[u][j].astype(BF16), preferred_element_type=F32)

    def inv(u, j):
        y = jnp.dot(dftt_ref[j], a_refs[u][j].astype(BF16), preferred_element_type=F32)
        a_refs[u][j, 0:K, :] = y

    def freq(u, order, side_work):
        a_ref = a_refs[u]
        pieces = [(r0, slice(lc * LANES, (lc + 1) * LANES))
                  for r0 in range(0, K, SUBLANES) for lc in range(ct // LANES)]
        for g in range(len(pieces) // FREQ_GROUP):
            if g % groups_per_side_op == 0:
                side_work(g // groups_per_side_op)
            group = pieces[g * FREQ_GROUP:(g + 1) * FREQ_GROUP]
            results = []
            for r0, cols in group:
                spec = _small_dft([_load_complex(a_ref, (j,), r0, cols, K) for j in range(DFT_N2)], -1)
                prod = []
                for k2 in range(DFT_N2):
                    kr, ki = _load_complex(kf_ref, (order, k2), r0, cols, K)
                    xr, xi = spec[k2]
                    prod.append((xr * kr - xi * ki, xr * ki + xi * kr))
                results.append(_small_dft(prod, +1))
            for (r0, cols), back in zip(group, results):
                for j in range(DFT_N2):
                    a_ref[j, r0:r0 + SUBLANES, cols] = back[j][0]
                    a_ref[j, K + r0:K + r0 + SUBLANES, cols] = back[j][1]

    def gate(u, order, gates, j):
        val = gates[j] * (a_refs[u][j, 0:K, :] + bias[order:order + 1, :] * sig_refs[u][j])
        if order + 1 < HYENA_ORDER:
            sig_refs[u][j] = val
        else:
            o_ref[u, j] = val.astype(o_ref.dtype)

    A, B = 0, 1
    for u in (A, B):
        for j, v in enumerate(conv(hv_ref, u, 0)):
            sig_refs[u][j] = v
    for order, gate_src in enumerate((hx1_ref, hx2_ref)):
        for j in range(DFT_N2):
            fwd(A, j)
        freq(A, order, lambda i: fwd(B, i))
        freq(B, order, lambda i: inv(A, i))
        gates = conv(gate_src, A, 1 + order)
        for j in range(DFT_N2):
            inv(B, j)
            gate(A, order, gates, j)
        gates = conv(gate_src, B, 1 + order)
        for j in range(DFT_N2):
            gate(B, order, gates, j)


def _hyena_branch(hz, conv_w, conv_b, hy_bias, kf, dft, dftt, *, ct):
    B, _, K, c3 = hz.shape
    C = c3 // 3
    nct = C // ct
    nb = 2
    cw = conv_w.reshape(3, 3, nct, ct).transpose(0, 2, 1, 3).reshape(3, nct * 3 * ct)
    cb = conv_b.reshape(3, nct, ct).transpose(1, 0, 2).reshape(1, nct * 3 * ct)
    hz_spec = lambda part: pl.BlockSpec((nb, DFT_N2, K, ct), lambda c, b: (b, 0, 0, part * nct + c))
    once = pl.Buffered(1)
    return pl.pallas_call(
        functools.partial(_hyena_kernel, K=K),
        grid=(nct, B // nb),
        in_specs=[
            hz_spec(0), hz_spec(1), hz_spec(2),
            pl.BlockSpec((3, 3 * ct), lambda c, b: (0, c)),
            pl.BlockSpec((1, 3 * ct), lambda c, b: (0, c)),
            pl.BlockSpec((HYENA_ORDER, ct), lambda c, b: (0, c)),
            pl.BlockSpec((HYENA_ORDER, DFT_N2, 2 * K, ct), lambda c, b: (0, 0, 0, c), pipeline_mode=once),
            pl.BlockSpec((DFT_N2, 2 * K, K), lambda c, b: (0, 0, 0), pipeline_mode=once),
            pl.BlockSpec((DFT_N2, K, 2 * K), lambda c, b: (0, 0, 0), pipeline_mode=once),
        ],
        out_specs=pl.BlockSpec((nb, DFT_N2, K, ct), lambda c, b: (b, 0, 0, c)),
        out_shape=jax.ShapeDtypeStruct((B, DFT_N2, K, C), BF16),
        scratch_shapes=[pltpu.VMEM((DFT_N2, K, ct), F32)] * nb + [pltpu.VMEM((DFT_N2, 2 * K, ct), F32)] * nb,
        compiler_params=_params("parallel", "arbitrary"),
    )(hz, hz, hz, cw, cb, hy_bias, kf, dft, dftt)


def _split2(v):
    a = v.astype(BF16)
    return a, (v - a.astype(F32)).astype(BF16)


def _to_natural(pt_ref, blk_ref, k):
    d, _, w = blk_ref.shape
    rows = PERM_TILE // d
    blk = blk_ref[:, k * rows:(k + 1) * rows, :].reshape(PERM_TILE, w)
    if blk.dtype == BF16:
        return jnp.dot(pt_ref[...], blk, preferred_element_type=F32)
    pieces = jnp.concatenate(_split2(blk), axis=-1)
    moved = jnp.dot(pt_ref[...], pieces, preferred_element_type=F32)
    return moved[:, :w] + moved[:, w:]


def _merge_kernel(x_ref, gpre_ref, o0_ref, o1_ref, o2_ref, l0_ref, l1_ref, l2_ref, z_ref,
                  pt1_ref, pt2_ref, ptz_ref, expand_ref,
                  wg_ref, bg_ref, woa_ref, woh_ref, wout_ref, gpost_ref, out_ref, oatt_ref, zn_ref):
    tm, D = x_ref.shape
    for k in range(tm // PERM_TILE):
        rows = slice(k * PERM_TILE, (k + 1) * PERM_TILE)
        lses = [l0_ref[rows, :], _to_natural(pt1_ref, l1_ref, k), _to_natural(pt2_ref, l2_ref, k)]
        outs = [o0_ref[rows, :].astype(F32), _to_natural(pt1_ref, o1_ref, k), _to_natural(pt2_ref, o2_ref, k)]
        m = jnp.maximum(jnp.maximum(lses[0], lses[1]), lses[2])
        e = [jnp.exp(l - m) for l in lses]
        inv = 1.0 / (e[0] + e[1] + e[2])
        o_att = jnp.zeros((PERM_TILE, GROUP_W), F32)
        for g in range(N_GROUPS):
            alpha = e[g] * inv
            wide = jnp.dot(alpha.astype(BF16), expand_ref[...], preferred_element_type=F32)
            o_att = o_att + wide * outs[g]
        oatt_ref[rows, :] = o_att.astype(BF16)
        zn_ref[rows, :] = _to_natural(ptz_ref, z_ref, k).astype(BF16)

    x = x_ref[...]
    u = _rms(x, gpre_ref[...]).astype(BF16)
    y_att = jnp.dot(oatt_ref[...], woa_ref[...], preferred_element_type=F32)
    y_hy = jnp.dot(zn_ref[...], woh_ref[...], preferred_element_type=F32)
    gate = lambda half: jax.nn.sigmoid(
        jnp.dot(u, wg_ref[:, half * D:(half + 1) * D], preferred_element_type=F32)
        + bg_ref[:, half * D:(half + 1) * D])
    merged = gate(0) * y_att + gate(1) * y_hy
    mix = jnp.dot(merged.astype(BF16), wout_ref[...], preferred_element_type=F32)
    out_ref[...] = x + _rms(mix, gpost_ref[...])


def _merge(x, gpre, o_groups, lse_groups, z, wg, bg, woa, woh, wout, gpost, *, layer, tm):
    B, S, D = x.shape
    C = z.shape[-1]
    dils = [d for _, d in ATT_GROUPS]

    def rmajor(a, d):
        w = a.shape[-1]
        return a.reshape(B, d, S // d, w), pl.BlockSpec((None, d, tm // d, w), lambda b, i: (b, 0, i, 0))

    nat = lambda w: pl.BlockSpec((None, tm, w), lambda b, i: (b, i, 0))
    full = lambda a: pl.BlockSpec(a.shape, lambda b, i: (0,) * a.ndim, pipeline_mode=pl.Buffered(1))
    stacked = lambda a: pl.BlockSpec((None,) + a.shape[1:], lambda b, i: (layer,) + (0,) * (a.ndim - 1),
                                     pipeline_mode=pl.Buffered(1))
    o1, o1_spec = rmajor(o_groups[1], dils[1])
    o2, o2_spec = rmajor(o_groups[2], dils[2])
    l1, l1_spec = rmajor(lse_groups[1], dils[1])
    l2, l2_spec = rmajor(lse_groups[2], dils[2])
    z_spec = pl.BlockSpec((None, DFT_N2, tm // DFT_N2, C), lambda b, i: (b, 0, i, 0))
    pts = [jnp.asarray(_perm_matrix(d).T, dtype=BF16) for d in (dils[1], dils[2], DFT_N2)]
    expand = np.zeros((HEAD_DIM, GROUP_W), np.float32)
    for h in range(HEADS):
        expand[h * LSE_LANES, h * HEAD_DIM:(h + 1) * HEAD_DIM] = 1.0
    expand = jnp.asarray(expand, dtype=BF16)
    consts = pts + [expand, wg, bg, woa, woh, wout, gpost]
    const_specs = [full(a) for a in pts + [expand]] + [stacked(wg), full(bg), stacked(woa), stacked(woh),
                                                       stacked(wout), full(gpost)]
    return pl.pallas_call(
        _merge_kernel,
        grid=(B, S // tm),
        in_specs=[nat(D), full(gpre), nat(GROUP_W), o1_spec, o2_spec, nat(HEAD_DIM), l1_spec, l2_spec, z_spec]
                 + const_specs,
        out_specs=nat(D),
        out_shape=jax.ShapeDtypeStruct((B, S, D), F32),
        scratch_shapes=[pltpu.VMEM((tm, GROUP_W), BF16), pltpu.VMEM((tm, C), BF16)],
        compiler_params=_params("parallel", "parallel"),
    )(x, gpre, o_groups[0], o1, o2, lse_groups[0], l1, l2, z, *consts)


def _ffn_kernel(x_ref, gpre_ref, wgu_ref, wdown_ref, gpost_ref, out_ref, *, n_chunks):
    x = x_ref[...]
    u = _rms(x, gpre_ref[...]).astype(BF16)
    dff = wdown_ref.shape[0]
    ck = dff // n_chunks
    acc = jnp.zeros(x.shape, F32)
    for c in range(n_chunks):
        a = jnp.dot(u, wgu_ref[:, c * ck:(c + 1) * ck], preferred_element_type=F32)
        b = jnp.dot(u, wgu_ref[:, dff + c * ck:dff + (c + 1) * ck], preferred_element_type=F32)
        h = (a * jax.nn.sigmoid(a) * b).astype(BF16)
        acc = acc + jnp.dot(h, wdown_ref[c * ck:(c + 1) * ck, :], preferred_element_type=F32)
    out_ref[...] = x + _rms(acc, gpost_ref[...])


def _ffn(x, gpre, wgu, wdown, gpost, *, layer, tm, n_chunks):
    B, S, D = x.shape
    T = B * S
    full = lambda a: pl.BlockSpec(a.shape, lambda i: (0,) * a.ndim, pipeline_mode=pl.Buffered(1))
    stacked = lambda a: pl.BlockSpec((None,) + a.shape[1:], lambda i: (layer,) + (0,) * (a.ndim - 1),
                                     pipeline_mode=pl.Buffered(1))
    out = pl.pallas_call(
        functools.partial(_ffn_kernel, n_chunks=n_chunks),
        grid=(T // tm,),
        in_specs=[pl.BlockSpec((tm, D), lambda i: (i, 0)), full(gpre), stacked(wgu), stacked(wdown), full(gpost)],
        out_specs=pl.BlockSpec((tm, D), lambda i: (i, 0)),
        out_shape=jax.ShapeDtypeStruct((T, D), F32),
        compiler_params=_params("parallel"),
    )(x.reshape(T, D), gpre, wgu, wdown, gpost)
    return out.reshape(B, S, D)


def kernel(x, norm_mix_pre, norm_mix_post, norm_ffn_pre, norm_ffn_post, w_in, conv_w, conv_b, filt_w1, filt_b1, filt_freq1, filt_w2, filt_b2, filt_freq2, filt_w3, hyena_bias, w_o_attn, w_o_hyena, w_gate, b_gate, w_out, w_gate_up, w_down):
    B, S, D = x.shape
    depth = w_in.shape[0]
    C = w_o_hyena.shape[1]
    K = S // DFT_N2
    assert S % PERM_TILE == 0 and w_in.shape[2] == 3 * ATT_W + 3 * C

    tables = {dil: [t.reshape(S // dil, dil, HEAD_DIM).transpose(1, 0, 2).reshape(S, HEAD_DIM)
                    for t in _rotary_tables(S)] for _, dil in ATT_GROUPS}
    dft, dftt = _dft_matrices(S)
    row = lambda v: v.reshape(1, -1)
    hy_first = 3 * ATT_W // GROUP_W
    hy_blocks_per_step = 3 * C // GROUP_W // 2

    w_in_b = w_in
    w_gate_b, w_oa_b, w_oh_b, w_out_b, w_gu_b, w_down_b = (
        w.astype(BF16) for w in (w_gate, w_o_attn, w_o_hyena, w_out, w_gate_up, w_down))

    for l in range(depth):
        g_pre = row(norm_mix_pre[l])
        groups = []
        for g, (_, dil) in enumerate(ATT_GROUPS):
            qkv = _proj(x, g_pre, w_in_b, layer=l, dil=dil, tw=GROUP_W, steps=1,
                        first_blocks=[g, N_GROUPS + g, 2 * N_GROUPS + g], rotary_tables=tables[dil])
            groups.append(_attention_group(qkv, dil))
        hz = _proj(x, g_pre, w_in_b, layer=l, dil=DFT_N2, tw=GROUP_W, steps=2,
                   first_blocks=[hy_first + i for i in range(hy_blocks_per_step)])
        hz = hz.reshape(B, DFT_N2, K, 3 * C)

        kf = _filter_spectrum(S, C, filt_w1[l], filt_b1[l], filt_freq1[l], filt_w2[l], filt_b2[l],
                              filt_freq2[l], filt_w3[l], dft, ct=256)
        z = _hyena_branch(hz, conv_w[l], conv_b[l], hyena_bias[l], kf, dft, dftt, ct=256)

        x = _merge(x, g_pre, [o for o, _ in groups], [s for _, s in groups], z,
                   w_gate_b, row(b_gate[l]), w_oa_b, w_oh_b, w_out_b, row(norm_mix_post[l]), layer=l, tm=512)
        x = _ffn(x, row(norm_ffn_pre[l]), w_gu_b, w_down_b, row(norm_ffn_post[l]), layer=l, tm=512, n_chunks=2)
    return x
```

```python
import functools
import math

import numpy as np
import jax
import jax.numpy as jnp
from jax import lax
from jax.experimental import pallas as pl
from jax.experimental.pallas import tpu as pltpu

F32 = jnp.float32
BF16 = jnp.bfloat16

HEAD_DIM = 128
HEADS = 4
ATT_GROUPS = ((128, 1), (512, 4), (2048, 16))
N_GROUPS = len(ATT_GROUPS)
GROUP_W = HEADS * HEAD_DIM
ATT_W = N_GROUPS * GROUP_W
ROPE_DIM = HEAD_DIM // 4
ROPE_HALF = ROPE_DIM // 2
ROPE_THETA = 500000.0
NEG_INF = -1e30
FILTER_EMB = 33
FILTER_HIDDEN = 64
HYENA_ORDER = 2
DECAY_TARGET = 1e-2
FAST_DECAY_PCT = 0.3
SLOW_DECAY_PCT = 1.5
RMS_EPS = 1e-6

LANES = 128
SUBLANES = 8
VMEM_LIMIT_BYTES = 56 * 1024 * 1024

DFT_N2 = 8
FREQ_GROUP = 8
Q_BLOCK = 128
HALF_WIN = 64
PERM_TILE = 256
LSE_LANES = HEAD_DIM // HEADS
ROTARY_ROWS = 512

def _params(*sem):
    return pltpu.CompilerParams(dimension_semantics=sem, vmem_limit_bytes=VMEM_LIMIT_BYTES)


def _rms(x, gain):
    return x * lax.rsqrt(jnp.mean(x * x, axis=-1, keepdims=True) + RMS_EPS) * gain


def _perm_matrix(dil):
    rows = PERM_TILE // dil
    p = np.zeros((PERM_TILE, PERM_TILE), np.float32)
    for r in range(dil):
        for t in range(rows):
            p[r * rows + t, dil * t + r] = 1.0
    return p


def _proj_kernel(x_ref, g_ref, p_ref, *refs, dil, rotary):
    if rotary:
        (c_ref, sa_ref, sb_ref), refs = refs[:3], refs[3:]
    w_refs, o_ref, u_ref = refs[:-2], refs[-2], refs[-1]
    S, D = x_ref.shape
    ls = S // dil
    rows = PERM_TILE // dil
    tw = w_refs[0].shape[1]

    @pl.when(pl.program_id(1) == 0)
    def _():
        for k in range(S // PERM_TILE):
            u = _rms(x_ref[k * PERM_TILE:(k + 1) * PERM_TILE, :], g_ref[...]).astype(BF16)
            if dil == 1:
                u_ref[k * PERM_TILE:(k + 1) * PERM_TILE, :] = u
            else:
                up = jnp.dot(p_ref[...], u, preferred_element_type=F32).astype(BF16)
                for r in range(dil):
                    u_ref[r * ls + k * rows:r * ls + (k + 1) * rows, :] = up[r * rows:(r + 1) * rows]

    for i, w_ref in enumerate(w_refs):
        res = jnp.dot(u_ref[...], w_ref[...].astype(BF16), preferred_element_type=F32)
        if rotary and i < 2:
            post = HEAD_DIM ** -0.5 if i == 0 else 1.0
            for h in range(tw // HEAD_DIM):
                for r0 in range(0, S, ROTARY_ROWS):
                    rows = slice(r0, r0 + ROTARY_ROWS)
                    tile = _rotate(res[rows, h * HEAD_DIM:(h + 1) * HEAD_DIM],
                                   c_ref[rows, :], sa_ref[rows, :], sb_ref[rows, :])
                    o_ref[rows, i * tw + h * HEAD_DIM:i * tw + (h + 1) * HEAD_DIM] = (tile * post).astype(o_ref.dtype)
        else:
            o_ref[:, i * tw:(i + 1) * tw] = res.astype(o_ref.dtype)


def _proj(x, gain, w_bf16, *, layer, dil, tw, first_blocks, steps, rotary_tables=None):
    B, S, D = x.shape
    n_w = len(first_blocks)
    perm = jnp.asarray(_perm_matrix(dil), dtype=BF16)
    w_spec = lambda f: pl.BlockSpec((None, D, tw), lambda b, c: (layer, 0, f + c * n_w))
    tables = list(rotary_tables or ())
    table_spec = pl.BlockSpec((S, HEAD_DIM), lambda b, c: (0, 0), pipeline_mode=pl.Buffered(1))
    return pl.pallas_call(
        functools.partial(_proj_kernel, dil=dil, rotary=bool(tables)),
        grid=(B, steps),
        in_specs=[
            pl.BlockSpec((None, S, D), lambda b, c: (b, 0, 0)),
            pl.BlockSpec((1, D), lambda b, c: (0, 0)),
            pl.BlockSpec((PERM_TILE, PERM_TILE), lambda b, c: (0, 0)),
        ] + [table_spec] * len(tables) + [w_spec(f) for f in first_blocks],
        out_specs=pl.BlockSpec((None, S, n_w * tw), lambda b, c: (b, 0, c)),
        out_shape=jax.ShapeDtypeStruct((B, S, steps * n_w * tw), BF16),
        scratch_shapes=[pltpu.VMEM((S, D), BF16)],
        compiler_params=_params("parallel", "arbitrary"),
    )(x, gain, perm, *tables, *([w_bf16] * n_w))


def _rotary_tables(S):
    pos = jnp.arange(S, dtype=jnp.int32)
    inv = ROPE_THETA ** (-jnp.arange(0, ROPE_DIM, 2, dtype=F32) / ROPE_DIM)
    ang = pos.astype(F32)[:, None] * inv[None, :]
    cos, sin = jnp.cos(ang), jnp.sin(ang)
    rest = HEAD_DIM - ROPE_DIM
    c = jnp.concatenate([cos, cos, jnp.ones((S, rest), F32)], axis=-1)
    sa = jnp.concatenate([-sin, jnp.zeros((S, HEAD_DIM - ROPE_HALF), F32)], axis=-1)
    sb = jnp.concatenate([jnp.zeros((S, ROPE_HALF), F32), sin, jnp.zeros((S, rest), F32)], axis=-1)
    return c, sa, sb


def _rotate(x, c, sa, sb):
    return x * c + pltpu.roll(x, HEAD_DIM - ROPE_HALF, 1) * sa + pltpu.roll(x, ROPE_HALF, 1) * sb


def _attn_kernel(q_ref, k_ref, v_ref, o_ref, lse_ref, vx_ref, *, ls):
    S = q_ref.shape[0]
    kw = min(2 * Q_BLOCK, ls)
    blocks_per_seq = ls // Q_BLOCK
    nblk = S // Q_BLOCK
    head_cols = [slice(h * HEAD_DIM, (h + 1) * HEAD_DIM) for h in range(HEADS)]

    def prepare(n):
        rows = slice(n * Q_BLOCK, (n + 1) * Q_BLOCK)
        ones = jnp.ones((Q_BLOCK, HEAD_DIM), BF16)
        for h, cols in enumerate(head_cols):
            vx_ref[rows, 2 * h * HEAD_DIM:(2 * h + 1) * HEAD_DIM] = v_ref[rows, cols]
            vx_ref[rows, (2 * h + 1) * HEAD_DIM:(2 * h + 2) * HEAD_DIM] = ones

    row = lax.broadcasted_iota(jnp.int32, (Q_BLOCK, kw), 0)
    col = lax.broadcasted_iota(jnp.int32, (Q_BLOCK, kw), 1)
    rel = col - row
    lane = lax.broadcasted_iota(jnp.int32, (Q_BLOCK, HEAD_DIM), 1)

    def block(n):
        base = (n // blocks_per_seq) * ls
        i0 = n * Q_BLOCK
        k0 = base + min(max(i0 - base - HALF_WIN, 0), ls - kw)
        d = rel + (k0 - i0)
        valid = (d <= HALF_WIN) & (d >= -HALF_WIN)
        qrows = slice(i0, i0 + Q_BLOCK)
        krows = slice(k0, k0 + kw)
        scores = [lax.dot_general(q_ref[qrows, cols], k_ref[krows, cols], (((1,), (1,)), ((), ())),
                                  preferred_element_type=F32) for cols in head_cols]
        if n + 2 < nblk:
            prepare(n + 2)
        maxes, probs = [], []
        for s in scores:
            s = jnp.where(valid, s, NEG_INF)
            m = jnp.max(s, axis=-1, keepdims=True)
            maxes.append(m)
            probs.append(jnp.exp(s - m).astype(BF16))
        pv = [jnp.dot(p, vx_ref[krows, 2 * h * HEAD_DIM:(2 * h + 2) * HEAD_DIM], preferred_element_type=F32)
              for h, p in enumerate(probs)]
        lse_tile = jnp.zeros((Q_BLOCK, HEAD_DIM), F32)
        for h in range(HEADS):
            l = pv[h][:, HEAD_DIM:]
            o_ref[qrows, head_cols[h]] = (pv[h][:, :HEAD_DIM] / l).astype(o_ref.dtype)
            lse_tile = jnp.where(lane // LSE_LANES == h, maxes[h] + jnp.log(l), lse_tile)
        lse_ref[qrows, :] = lse_tile

    prepare(0)
    prepare(1)
    for n in range(nblk):
        block(n)


def _attention_group(qkv, dil):
    B, S, _ = qkv.shape
    part = lambda which: pl.BlockSpec((None, S, GROUP_W), lambda b: (b, 0, which))
    return pl.pallas_call(
        functools.partial(_attn_kernel, ls=S // dil),
        grid=(B,),
        in_specs=[part(0), part(1), part(2)],
        out_specs=[pl.BlockSpec((None, S, GROUP_W), lambda b: (b, 0, 0)),
                   pl.BlockSpec((None, S, HEAD_DIM), lambda b: (b, 0, 0))],
        out_shape=[jax.ShapeDtypeStruct((B, S, GROUP_W), BF16),
                   jax.ShapeDtypeStruct((B, S, HEAD_DIM), F32)],
        scratch_shapes=[pltpu.VMEM((S, 2 * GROUP_W), BF16)],
        compiler_params=_params("parallel"),
    )(qkv, qkv, qkv)


def _dft_matrices(S):
    N = 2 * S
    K = S // DFT_N2
    k1 = np.arange(K, dtype=np.float64)[:, None] + 0.5
    mats = []
    for j in range(DFT_N2):
        n = DFT_N2 * np.arange(K, dtype=np.float64)[None, :] + j
        theta = 2.0 * np.pi * k1 * n / N
        mats.append(np.concatenate([np.cos(theta), -np.sin(theta)], axis=0))
    f = np.stack(mats).astype(np.float32)
    return jnp.asarray(f).astype(BF16), jnp.asarray(f.transpose(0, 2, 1)).astype(BF16)


def _cmul_const(v, c, s):
    re, im = v
    if (c, s) == (1, 0):
        return re, im
    if (c, s) == (-1, 0):
        return -re, -im
    if (c, s) == (0, 1):
        return -im, re
    if (c, s) == (0, -1):
        return im, -re
    if abs(abs(c) - abs(s)) < 1e-12:
        sc, ss = math.copysign(1.0, c), math.copysign(1.0, s)
        a = re - im if sc * ss > 0 else re + im
        b = re + im if sc * ss > 0 else im - re
        return a * (sc * abs(c)), b * (sc * abs(c))
    return re * c - im * s, re * s + im * c


def _small_dft(xs, sign):
    n = len(xs)
    if n == 1:
        return xs
    ev = _small_dft(xs[0::2], sign)
    od = _small_dft(xs[1::2], sign)
    out = [None] * n
    for k in range(n // 2):
        c = round(math.cos(2.0 * math.pi * k / n), 15)
        s = round(sign * math.sin(2.0 * math.pi * k / n), 15)
        c = int(c) if c in (-1.0, 0.0, 1.0) else c
        s = int(s) if s in (-1.0, 0.0, 1.0) else s
        tr, ti = _cmul_const(od[k], c, s)
        out[k] = (ev[k][0] + tr, ev[k][1] + ti)
        out[k + n // 2] = (ev[k][0] - tr, ev[k][1] - ti)
    return out


def _load_complex(ref, lead, r0, cols, K):
    re = ref[lead + (pl.ds(r0, SUBLANES), cols)]
    im = ref[lead + (pl.ds(K + r0, SUBLANES), cols)]
    return re, im


def _filter_kernel(z_ref, w1_ref, b1_ref, f1_ref, w2_ref, b2_ref, f2_ref, w3_ref, delta_ref, dft_ref,
                   kf_ref, h_ref, a_ref, *, K):
    ct = kf_ref.shape[-1]
    hi = lax.Precision.HIGHEST

    @pl.when(pl.program_id(0) == 0)
    def _():
        for j in range(DFT_N2):
            z = z_ref[j]
            h = jnp.sin(f1_ref[...] * (jnp.dot(z, w1_ref[...], precision=hi, preferred_element_type=F32)
                                       + b1_ref[...]))
            h = jnp.sin(f2_ref[...] * (jnp.dot(h, w2_ref[...], precision=hi, preferred_element_type=F32)
                                       + b2_ref[...]))
            h_ref[j] = h.astype(BF16)

    inv_scale = 2.0 / (2 * DFT_N2 * K)
    for o in range(HYENA_ORDER):
        for j in range(DFT_N2):
            decay = jnp.exp(-z_ref[j, :, 0:1] * jnp.abs(delta_ref[...]))
            for d in range(2):
                hf = jnp.dot(h_ref[j], w3_ref[2 * o + d], preferred_element_type=F32) * decay
                a_ref[d, j] = jnp.dot(dft_ref[j], hf.astype(BF16), preferred_element_type=F32)

        def chunk(i, carry):
            r0 = pl.multiple_of(i * SUBLANES, SUBLANES)
            for lc in range(ct // LANES):
                cols = slice(lc * LANES, (lc + 1) * LANES)
                fw = _small_dft([_load_complex(a_ref, (0, j), r0, cols, K) for j in range(DFT_N2)], -1)
                bw = _small_dft([_load_complex(a_ref, (1, j), r0, cols, K) for j in range(DFT_N2)], -1)
                for k2 in range(DFT_N2):
                    kf_ref[o, k2, pl.ds(r0, SUBLANES), cols] = (fw[k2][0] + bw[k2][0]) * inv_scale
                    kf_ref[o, k2, pl.ds(K + r0, SUBLANES), cols] = (fw[k2][1] - bw[k2][1]) * inv_scale
            return carry

        lax.fori_loop(0, K // SUBLANES, chunk, 0)


def _filter_spectrum(S, C, fw1, fb1, ff1, fw2, fb2, ff2, fw3, dft, *, ct):
    K = S // DFT_N2
    bands = (FILTER_EMB - 1) // 2
    t = jnp.linspace(0.0, 1.0, S, dtype=F32)[:, None]
    w = 2.0 * math.pi * jnp.arange(S, dtype=F32)[:, None] / S
    fr = jnp.linspace(1e-4, bands - 1, bands, dtype=F32)[None, :]
    feats = jnp.concatenate([t, jnp.cos(fr * w), -jnp.sin(fr * w)], axis=-1)
    feats = jnp.pad(feats, ((0, 0), (0, LANES - FILTER_EMB)))
    feats = feats.reshape(K, DFT_N2, LANES).transpose(1, 0, 2)
    w1 = jnp.pad(fw1, ((0, LANES - FILTER_EMB), (0, 0)))
    w3 = fw3.reshape(FILTER_HIDDEN, 2 * HYENA_ORDER, C).transpose(1, 0, 2).astype(BF16)
    deltas = jnp.linspace(math.log(DECAY_TARGET) / SLOW_DECAY_PCT, math.log(DECAY_TARGET) / FAST_DECAY_PCT,
                          C, dtype=F32)[None, :]
    row = lambda v: v.reshape(1, -1)
    const2 = lambda shape: pl.BlockSpec(shape, lambda c: (0, 0))
    return pl.pallas_call(
        functools.partial(_filter_kernel, K=K),
        grid=(C // ct,),
        in_specs=[
            pl.BlockSpec((DFT_N2, K, LANES), lambda c: (0, 0, 0)),
            const2((LANES, FILTER_HIDDEN)), const2((1, FILTER_HIDDEN)), const2((1, FILTER_HIDDEN)),
            const2((FILTER_HIDDEN, FILTER_HIDDEN)), const2((1, FILTER_HIDDEN)), const2((1, FILTER_HIDDEN)),
            pl.BlockSpec((2 * HYENA_ORDER, FILTER_HIDDEN, ct), lambda c: (0, 0, c)),
            pl.BlockSpec((1, ct), lambda c: (0, c)),
            pl.BlockSpec((DFT_N2, 2 * K, K), lambda c: (0, 0, 0)),
        ],
        out_specs=pl.BlockSpec((HYENA_ORDER, DFT_N2, 2 * K, ct), lambda c: (0, 0, 0, c)),
        out_shape=jax.ShapeDtypeStruct((HYENA_ORDER, DFT_N2, 2 * K, C), F32),
        scratch_shapes=[pltpu.VMEM((DFT_N2, K, FILTER_HIDDEN), BF16),
                        pltpu.VMEM((2, DFT_N2, 2 * K, ct), F32)],
        compiler_params=_params("arbitrary"),
    )(feats, w1, row(fb1), row(ff1), fw2, row(fb2), row(ff2), w3, deltas, dft)


def _short_conv(p_ref, w, b):
    K = p_ref.shape[1]
    p = [p_ref[j].astype(F32) for j in range(DFT_N2)]
    rows = lax.broadcasted_iota(jnp.int32, p[0].shape, 0)
    prev_of_first = jnp.where(rows == 0, 0.0, pltpu.roll(p[DFT_N2 - 1], 1, 0))
    next_of_last = jnp.where(rows == K - 1, 0.0, pltpu.roll(p[0], K - 1, 0))
    out = []
    for j in range(DFT_N2):
        prev = p[j - 1] if j > 0 else prev_of_first
        nxt = p[j + 1] if j < DFT_N2 - 1 else next_of_last
        out.append(prev * w[0:1, :] + p[j] * w[1:2, :] + nxt * w[2:3, :] + b)
    return out


def _hyena_kernel(hv_ref, hx1_ref, hx2_ref, cw_ref, cb_ref, bias_ref, kf_ref, dft_ref, dftt_ref,
                  o_ref, sig0_ref, sig1_ref, a0_ref, a1_ref, *, K):
    n_units, _, _, ct = o_ref.shape
    assert n_units == 2
    sig_refs, a_refs = (sig0_ref, sig1_ref), (a0_ref, a1_ref)
    cw = cw_ref[...]
    cb = cb_ref[...]
    bias = bias_ref[...]
    conv = lambda src, u, idx: _short_conv(src.at[u], cw[:, idx * ct:(idx + 1) * ct], cb[:, idx * ct:(idx + 1) * ct])
    groups_per_side_op = (K // SUBLANES) * (ct // LANES) // FREQ_GROUP // DFT_N2

    def fwd(u, j):
        a_refs[u][j] = jnp.dot(dft_ref[j], sig_refs[u][j].astype(BF16), preferred_element_type=F32)

    def inv(u, j):
        y = jnp.dot(dftt_ref[j], a_refs[u][j].astype(BF16), preferred_element_type=F32)
        a_refs[u][j, 0:K, :] = y

    def freq(u, order, side_work):
        a_ref = a_refs[u]
        pieces = [(r0, slice(lc * LANES, (lc + 1) * LANES))
                  for r0 in range(0, K, SUBLANES) for lc in range(ct // LANES)]
        for g in range(len(pieces) // FREQ_GROUP):
            if g % groups_per_side_op == 0:
                side_work(g // groups_per_side_op)
            group = pieces[g * FREQ_GROUP:(g + 1) * FREQ_GROUP]
            results = []
            for r0, cols in group:
                spec = _small_dft([_load_complex(a_ref, (j,), r0, cols, K) for j in range(DFT_N2)], -1)
                prod = []
                for k2 in range(DFT_N2):
                    kr, ki = _load_complex(kf_ref, (order, k2), r0, cols, K)
                    xr, xi = spec[k2]
                    prod.append((xr * kr - xi * ki, xr * ki + xi * kr))
                results.append(_small_dft(prod, +1))
            for (r0, cols), back in zip(group, results):
                for j in range(DFT_N2):
                    a_ref[j, r0:r0 + SUBLANES, cols] = back[j][0]
                    a_ref[j, K + r0:K + r0 + SUBLANES, cols] = back[j][1]

    def gate(u, order, gates, j):
        val = gates[j] * (a_refs[u][j, 0:K, :] + bias[order:order + 1, :] * sig_refs[u][j])
        if order + 1 < HYENA_ORDER:
            sig_refs[u][j] = val
        else:
            o_ref[u, j] = val.astype(o_ref.dtype)

    A, B = 0, 1
    for u in (A, B):
        for j, v in enumerate(conv(hv_ref, u, 0)):
            sig_refs[u][j] = v
    for order, gate_src in enumerate((hx1_ref, hx2_ref)):
        for j in range(DFT_N2):
            fwd(A, j)
        freq(A, order, lambda i: fwd(B, i))
        freq(B, order, lambda i: inv(A, i))
        gates = conv(gate_src, A, 1 + order)
        for j in range(DFT_N2):
            inv(B, j)
            gate(A, order, gates, j)
        gates = conv(gate_src, B, 1 + order)
        for j in range(DFT_N2):
            gate(B, order, gates, j)


def _hyena_branch(hz, conv_w, conv_b, hy_bias, kf, dft, dftt, *, ct):
    B, _, K, c3 = hz.shape
    C = c3 // 3
    nct = C // ct
    nb = 2
    cw = conv_w.reshape(3, 3, nct, ct).transpose(0, 2, 1, 3).reshape(3, nct * 3 * ct)
    cb = conv_b.reshape(3, nct, ct).transpose(1, 0, 2).reshape(1, nct * 3 * ct)
    hz_spec = lambda part: pl.BlockSpec((nb, DFT_N2, K, ct), lambda c, b: (b, 0, 0, part * nct + c))
    once = pl.Buffered(1)
    return pl.pallas_call(
        functools.partial(_hyena_kernel, K=K),
        grid=(nct, B // nb),
        in_specs=[
            hz_spec(0), hz_spec(1), hz_spec(2),
            pl.BlockSpec((3, 3 * ct), lambda c, b: (0, c)),
            pl.BlockSpec((1, 3 * ct), lambda c, b: (0, c)),
            pl.BlockSpec((HYENA_ORDER, ct), lambda c, b: (0, c)),
            pl.BlockSpec((HYENA_ORDER, DFT_N2, 2 * K, ct), lambda c, b: (0, 0, 0, c), pipeline_mode=once),
            pl.BlockSpec((DFT_N2, 2 * K, K), lambda c, b: (0, 0, 0), pipeline_mode=once),
            pl.BlockSpec((DFT_N2, K, 2 * K), lambda c, b: (0, 0, 0), pipeline_mode=once),
        ],
        out_specs=pl.BlockSpec((nb, DFT_N2, K, ct), lambda c, b: (b, 0, 0, c)),
        out_shape=jax.ShapeDtypeStruct((B, DFT_N2, K, C), BF16),
        scratch_shapes=[pltpu.VMEM((DFT_N2, K, ct), F32)] * nb + [pltpu.VMEM((DFT_N2, 2 * K, ct), F32)] * nb,
        compiler_params=_params("parallel", "arbitrary"),
    )(hz, hz, hz, cw, cb, hy_bias, kf, dft, dftt)


def _split2(v):
    a = v.astype(BF16)
    return a, (v - a.astype(F32)).astype(BF16)


def _to_natural(pt_ref, blk_ref, k):
    d, _, w = blk_ref.shape
    rows = PERM_TILE // d
    blk = blk_ref[:, k * rows:(k + 1) * rows, :].reshape(PERM_TILE, w)
    if blk.dtype == BF16:
        return jnp.dot(pt_ref[...], blk, preferred_element_type=F32)
    pieces = jnp.concatenate(_split2(blk), axis=-1)
    moved = jnp.dot(pt_ref[...], pieces, preferred_element_type=F32)
    return moved[:, :w] + moved[:, w:]


def _merge_kernel(x_ref, gpre_ref, o0_ref, o1_ref, o2_ref, l0_ref, l1_ref, l2_ref, z_ref,
                  pt1_ref, pt2_ref, ptz_ref, expand_ref,
                  wg_ref, bg_ref, woa_ref, woh_ref, wout_ref, gpost_ref, out_ref, oatt_ref, zn_ref):
    tm, D = x_ref.shape
    for k in range(tm // PERM_TILE):
        rows = slice(k * PERM_TILE, (k + 1) * PERM_TILE)
        lses = [l0_ref[rows, :], _to_natural(pt1_ref, l1_ref, k), _to_natural(pt2_ref, l2_ref, k)]
        outs = [o0_ref[rows, :].astype(F32), _to_natural(pt1_ref, o1_ref, k), _to_natural(pt2_ref, o2_ref, k)]
        m = jnp.maximum(jnp.maximum(lses[0], lses[1]), lses[2])
        e = [jnp.exp(l - m) for l in lses]
        inv = 1.0 / (e[0] + e[1] + e[2])
        o_att = jnp.zeros((PERM_TILE, GROUP_W), F32)
        for g in range(N_GROUPS):
            alpha = e[g] * inv
            wide = jnp.dot(alpha.astype(BF16), expand_ref[...], preferred_element_type=F32)
            o_att = o_att + wide * outs[g]
        oatt_ref[rows, :] = o_att.astype(BF16)
        zn_ref[rows, :] = _to_natural(ptz_ref, z_ref, k).astype(BF16)

    x = x_ref[...]
    u = _rms(x, gpre_ref[...]).astype(BF16)
    y_att = jnp.dot(oatt_ref[...], woa_ref[...], preferred_element_type=F32)
    y_hy = jnp.dot(zn_ref[...], woh_ref[...], preferred_element_type=F32)
    gate = lambda half: jax.nn.sigmoid(
        jnp.dot(u, wg_ref[:, half * D:(half + 1) * D], preferred_element_type=F32)
        + bg_ref[:, half * D:(half + 1) * D])
    merged = gate(0) * y_att + gate(1) * y_hy
    mix = jnp.dot(merged.astype(BF16), wout_ref[...], preferred_element_type=F32)
    out_ref[...] = x + _rms(mix, gpost_ref[...])


def _merge(x, gpre, o_groups, lse_groups, z, wg, bg, woa, woh, wout, gpost, *, layer, tm):
    B, S, D = x.shape
    C = z.shape[-1]
    dils = [d for _, d in ATT_GROUPS]

    def rmajor(a, d):
        w = a.shape[-1]
        return a.reshape(B, d, S // d, w), pl.BlockSpec((None, d, tm // d, w), lambda b, i: (b, 0, i, 0))

    nat = lambda w: pl.BlockSpec((None, tm, w), lambda b, i: (b, i, 0))
    full = lambda a: pl.BlockSpec(a.shape, lambda b, i: (0,) * a.ndim, pipeline_mode=pl.Buffered(1))
    stacked = lambda a: pl.BlockSpec((None,) + a.shape[1:], lambda b, i: (layer,) + (0,) * (a.ndim - 1),
                                     pipeline_mode=pl.Buffered(1))
    o1, o1_spec = rmajor(o_groups[1], dils[1])
    o2, o2_spec = rmajor(o_groups[2], dils[2])
    l1, l1_spec = rmajor(lse_groups[1], dils[1])
    l2, l2_spec = rmajor(lse_groups[2], dils[2])
    z_spec = pl.BlockSpec((None, DFT_N2, tm // DFT_N2, C), lambda b, i: (b, 0, i, 0))
    pts = [jnp.asarray(_perm_matrix(d).T, dtype=BF16) for d in (dils[1], dils[2], DFT_N2)]
    expand = np.zeros((HEAD_DIM, GROUP_W), np.float32)
    for h in range(HEADS):
        expand[h * LSE_LANES, h * HEAD_DIM:(h + 1) * HEAD_DIM] = 1.0
    expand = jnp.asarray(expand, dtype=BF16)
    consts = pts + [expand, wg, bg, woa, woh, wout, gpost]
    const_specs = [full(a) for a in pts + [expand]] + [stacked(wg), full(bg), stacked(woa), stacked(woh),
                                                       stacked(wout), full(gpost)]
    return pl.pallas_call(
        _merge_kernel,
        grid=(B, S // tm),
        in_specs=[nat(D), full(gpre), nat(GROUP_W), o1_spec, o2_spec, nat(HEAD_DIM), l1_spec, l2_spec, z_spec]
                 + const_specs,
        out_specs=nat(D),
        out_shape=jax.ShapeDtypeStruct((B, S, D), F32),
        scratch_shapes=[pltpu.VMEM((tm, GROUP_W), BF16), pltpu.VMEM((tm, C), BF16)],
        compiler_params=_params("parallel", "parallel"),
    )(x, gpre, o_groups[0], o1, o2, lse_groups[0], l1, l2, z, *consts)


def _ffn_kernel(x_ref, gpre_ref, wgu_ref, wdown_ref, gpost_ref, out_ref, *, n_chunks):
    x = x_ref[...]
    u = _rms(x, gpre_ref[...]).astype(BF16)
    dff = wdown_ref.shape[0]
    ck = dff // n_chunks
    acc = jnp.zeros(x.shape, F32)
    for c in range(n_chunks):
        a = jnp.dot(u, wgu_ref[:, c * ck:(c + 1) * ck], preferred_element_type=F32)
        b = jnp.dot(u, wgu_ref[:, dff + c * ck:dff + (c + 1) * ck], preferred_element_type=F32)
        h = (a * jax.nn.sigmoid(a) * b).astype(BF16)
        acc = acc + jnp.dot(h, wdown_ref[c * ck:(c + 1) * ck, :], preferred_element_type=F32)
    out_ref[...] = x + _rms(acc, gpost_ref[...])


def _ffn(x, gpre, wgu, wdown, gpost, *, layer, tm, n_chunks):
    B, S, D = x.shape
    T = B * S
    full = lambda a: pl.BlockSpec(a.shape, lambda i: (0,) * a.ndim, pipeline_mode=pl.Buffered(1))
    stacked = lambda a: pl.BlockSpec((None,) + a.shape[1:], lambda i: (layer,) + (0,) * (a.ndim - 1),
                                     pipeline_mode=pl.Buffered(1))
    out = pl.pallas_call(
        functools.partial(_ffn_kernel, n_chunks=n_chunks),
        grid=(T // tm,),
        in_specs=[pl.BlockSpec((tm, D), lambda i: (i, 0)), full(gpre), stacked(wgu), stacked(wdown), full(gpost)],
        out_specs=pl.BlockSpec((tm, D), lambda i: (i, 0)),
        out_shape=jax.ShapeDtypeStruct((T, D), F32),
        compiler_params=_params("parallel"),
    )(x.reshape(T, D), gpre, wgu, wdown, gpost)
    return out.reshape(B, S, D)


def kernel(x, norm_mix_pre, norm_mix_post, norm_ffn_pre, norm_ffn_post, w_in, conv_w, conv_b, filt_w1, filt_b1, filt_freq1, filt_w2, filt_b2, filt_freq2, filt_w3, hyena_bias, w_o_attn, w_o_hyena, w_gate, b_gate, w_out, w_gate_up, w_down):
    B, S, D = x.shape
    depth = w_in.shape[0]
    C = w_o_hyena.shape[1]
    K = S // DFT_N2
    assert S % PERM_TILE == 0 and w_in.shape[2] == 3 * ATT_W + 3 * C

    tables = {dil: [t.reshape(S // dil, dil, HEAD_DIM).transpose(1, 0, 2).reshape(S, HEAD_DIM)
                    for t in _rotary_tables(S)] for _, dil in ATT_GROUPS}
    dft, dftt = _dft_matrices(S)
    row = lambda v: v.reshape(1, -1)
    hy_first = 3 * ATT_W // GROUP_W
    hy_blocks_per_step = 3 * C // GROUP_W // 2

    w_in_b = w_in
    w_gate_b, w_oa_b, w_oh_b, w_out_b, w_gu_b, w_down_b = (
        w.astype(BF16) for w in (w_gate, w_o_attn, w_o_hyena, w_out, w_gate_up, w_down))

    for l in range(depth):
        g_pre = row(norm_mix_pre[l])
        groups = []
        for g, (_, dil) in enumerate(ATT_GROUPS):
            qkv = _proj(x, g_pre, w_in_b, layer=l, dil=dil, tw=GROUP_W, steps=1,
                        first_blocks=[g, N_GROUPS + g, 2 * N_GROUPS + g], rotary_tables=tables[dil])
            groups.append(_attention_group(qkv, dil))
        hz = _proj(x, g_pre, w_in_b, layer=l, dil=DFT_N2, tw=GROUP_W, steps=2,
                   first_blocks=[hy_first + i for i in range(hy_blocks_per_step)])
        hz = hz.reshape(B, DFT_N2, K, 3 * C)

        kf = _filter_spectrum(S, C, filt_w1[l], filt_b1[l], filt_freq1[l], filt_w2[l], filt_b2[l],
                              filt_freq2[l], filt_w3[l], dft, ct=256)
        z = _hyena_branch(hz, conv_w[l], conv_b[l], hyena_bias[l], kf, dft, dftt, ct=256)

        x = _merge(x, g_pre, [o for o, _ in groups], [s for _, s in groups], z,
                   w_gate_b, row(b_gate[l]), w_oa_b, w_oh_b, w_out_b, row(norm_mix_post[l]), layer=l, tm=512)
        x = _ffn(x, row(norm_ffn_pre[l]), w_gu_b, w_down_b, row(norm_ffn_post[l]), layer=l, tm=512, n_chunks=11)
    return x
```
